```python
import math
import jax, jax.numpy as jnp
from jax import lax
import numpy as np


D_MODEL = 2048
BATCH = 4
SEQ = 4096
DEPTH = 1

MEM_LEN = 256
EPS = 1e-6
DA_HEADS = 4
DA_HEAD_DIM = D_MODEL // (4 * DA_HEADS)
DA_WIDTH = DA_HEADS * 2 * DA_HEAD_DIM
Q_BLOCK = 128
HG_HEADS = 8
HG_KEY_DIM = 128
HG_VAL_DIM = D_MODEL // (2 * HG_HEADS)
HG_QK = HG_HEADS * HG_KEY_DIM
HG_WIDTH = HG_HEADS * HG_VAL_DIM
HG_CHUNK = 64
XA_HEADS = 4
XA_HEAD_DIM = D_MODEL // (2 * XA_HEADS)
XA_WIDTH = XA_HEADS * XA_HEAD_DIM
N_BRANCH = 3
IN_SIZES = (DA_WIDTH, DA_WIDTH, DA_WIDTH, HG_QK, HG_QK, HG_WIDTH, HG_WIDTH, XA_WIDTH, N_BRANCH * D_MODEL)
IN_COLS = 3 * DA_WIDTH + 2 * HG_QK + 2 * HG_WIDTH + XA_WIDTH + N_BRANCH * D_MODEL
D_FF = 5632

kernel_name = "hybrid_diffattn_hgrn2_memxattn_macaron"


def rmsnorm(x, g):
    xf = x.astype(jnp.float32)
    y = xf * lax.rsqrt(jnp.mean(xf * xf, axis=-1, keepdims=True) + EPS)
    return (y * g.astype(jnp.float32)).astype(x.dtype)


def swiglu(x, w_gate, w_up, w_down):
    return (jax.nn.silu(x @ w_gate) * (x @ w_up)) @ w_down


def alibi_slopes(n):
    return jnp.array([2.0 ** (-8.0 * (i + 1) / n) for i in range(n)], dtype=jnp.float32)


def split_cols(t, sizes):
    out, start = [], 0
    for n in sizes:
        out.append(t[..., start:start + n])
        start += n
    return out


def diff_attention(q, k, v, lam_q1, lam_k1, lam_q2, lam_k2, subln_g, lambda_init):
    B, S, _ = q.shape
    f32 = jnp.float32
    q = q.reshape(B, S, DA_HEADS, 2, DA_HEAD_DIM) * (DA_HEAD_DIM ** -0.5)
    k = k.reshape(B, S, DA_HEADS, 2, DA_HEAD_DIM)
    v = v.reshape(B, S, DA_HEADS, 2 * DA_HEAD_DIM)
    lam = (jnp.exp(jnp.sum(lam_q1.astype(f32) * lam_k1.astype(f32)))
           - jnp.exp(jnp.sum(lam_q2.astype(f32) * lam_k2.astype(f32))) + lambda_init)
    slopes = alibi_slopes(DA_HEADS)
    n_blocks = S // Q_BLOCK
    q_blocks = q.reshape(B, n_blocks, Q_BLOCK, DA_HEADS, 2, DA_HEAD_DIM).transpose(1, 0, 2, 3, 4, 5)
    k_pos = jnp.arange(S)

    def one_block(args):
        q_blk, blk = args
        s = jnp.einsum('bqhcd,bkhcd->bhcqk', q_blk, k, preferred_element_type=f32)
        q_pos = blk * Q_BLOCK + jnp.arange(Q_BLOCK)
        dist = q_pos[:, None] - k_pos[None, :]
        bias = -slopes[:, None, None] * dist.astype(f32)[None]
        s = jnp.where((dist >= 0)[None, None, None], s + bias[None, :, None], -jnp.inf)
        p = jax.nn.softmax(s, axis=-1)
        a = p[:, :, 0] - lam * p[:, :, 1]
        return jnp.einsum('bhqk,bkhe->bqhe', a.astype(v.dtype), v)

    o = lax.map(one_block, (q_blocks, jnp.arange(n_blocks)))
    o = o.transpose(1, 0, 2, 3, 4).reshape(B, S, DA_HEADS, 2 * DA_HEAD_DIM)
    o = rmsnorm(o, subln_g) * (1.0 - lambda_init)
    return o.reshape(B, S, DA_WIDTH)


def hgrn2(q, f_logit, i_in, o_gate, lb, norm_g):
    B, S, _ = q.shape
    dt = i_in.dtype
    f32 = jnp.float32
    z = f_logit.astype(f32).reshape(B, S, HG_HEADS, HG_KEY_DIM)
    lb = lb.astype(f32).reshape(HG_HEADS, HG_KEY_DIM)
    log_f = jnp.log(lb + (1.0 - lb) * jax.nn.sigmoid(z))
    kk = (1.0 - lb) * jax.nn.sigmoid(-z)
    qf = jax.nn.silu(q.astype(f32)).reshape(B, S, HG_HEADS, HG_KEY_DIM)
    v = i_in.astype(f32).reshape(B, S, HG_HEADS, HG_VAL_DIM)
    n_chunks = S // HG_CHUNK

    def chunked(t):
        return t.reshape(B, n_chunks, HG_CHUNK, HG_HEADS, t.shape[-1]).transpose(1, 0, 3, 2, 4)

    causal = jnp.tril(jnp.ones((HG_CHUNK, HG_CHUNK), dtype=bool))[:, :, None]

    def step(state, xs):
        qc, kc, gc, vc = xs
        b = lax.cumsum(gc, axis=2)
        pair = b[:, :, :, None, :] - b[:, :, None, :, :]
        decay = jnp.where(causal, jnp.exp(jnp.where(causal, pair, 0.0)), 0.0)
        scores = jnp.einsum('bhtsk,bhsk->bhts', qc[:, :, :, None, :] * decay, kc)
        o = (jnp.einsum('bhts,bhsv->bhtv', scores, vc)
             + jnp.einsum('bhtk,bhkv->bhtv', qc * jnp.exp(b), state))
        b_end = b[:, :, -1:, :]
        state = (jnp.exp(b_end[:, :, 0, :])[..., None] * state
                 + jnp.einsum('bhsk,bhsv->bhkv', kc * jnp.exp(b_end - b), vc))
        return state, o

    state0 = jnp.zeros((B, HG_HEADS, HG_KEY_DIM, HG_VAL_DIM), f32)
    _, o = lax.scan(step, state0, (chunked(qf), chunked(kk), chunked(log_f), chunked(v)))
    o = o.transpose(1, 0, 3, 2, 4).reshape(B, S, HG_HEADS, HG_VAL_DIM)
    o = o * lax.rsqrt(jnp.mean(o * o, axis=-1, keepdims=True) + EPS) * norm_g.astype(f32).reshape(HG_HEADS, HG_VAL_DIM)
    o = o * jax.nn.sigmoid(o_gate.astype(f32)).reshape(B, S, HG_HEADS, HG_VAL_DIM)
    return o.reshape(B, S, HG_WIDTH).astype(dt)


def memory_cross_attention(q, mem_n, w_mem_kv):
    B, S, _ = q.shape
    f32 = jnp.float32
    kv = mem_n @ w_mem_kv
    k, v = kv[..., :XA_WIDTH], kv[..., XA_WIDTH:]
    q = q.reshape(B, S, XA_HEADS, XA_HEAD_DIM) * (XA_HEAD_DIM ** -0.5)
    k = k.reshape(B, -1, XA_HEADS, XA_HEAD_DIM)
    v = v.reshape(B, -1, XA_HEADS, XA_HEAD_DIM)
    s = jnp.einsum('bqhd,bmhd->bhqm', q, k, preferred_element_type=f32)
    p = jax.nn.softmax(s, axis=-1)
    o = jnp.einsum('bhqm,bmhd->bqhd', p.astype(v.dtype), v)
    return o.reshape(B, S, XA_WIDTH)


def setup_inputs(seed: int = 0) -> dict:
    key = jax.random.key(seed)
    ks = jax.random.split(key, 32)
    f32 = jnp.float32
    L = DEPTH

    def nrm(k, shape, scale):
        return jax.random.normal(k, shape, f32) * scale

    def gain(k, shape):
        return 1.0 + 0.02 * jax.random.normal(k, shape, f32)

    return {
        'x': nrm(ks[0], (BATCH, SEQ, D_MODEL), 1.0),
        'mem': nrm(ks[1], (BATCH, MEM_LEN, D_MODEL), 1.0),
        'ffn1_norm': gain(ks[2], (L, D_MODEL)),
        'ffn1_w_gate': nrm(ks[3], (L, D_MODEL, D_FF), D_MODEL ** -0.5),
        'ffn1_w_up': nrm(ks[4], (L, D_MODEL, D_FF), D_MODEL ** -0.5),
        'ffn1_w_down': nrm(ks[5], (L, D_FF, D_MODEL), D_FF ** -0.5),
        'mix_norm': gain(ks[6], (L, D_MODEL)),
        'mem_norm': gain(ks[7], (L, D_MODEL)),
        'w_in': nrm(ks[8], (L, D_MODEL, IN_COLS), D_MODEL ** -0.5),
        'da_lambda_q1': nrm(ks[9], (L, DA_HEAD_DIM), 0.1),
        'da_lambda_k1': nrm(ks[10], (L, DA_HEAD_DIM), 0.1),
        'da_lambda_q2': nrm(ks[11], (L, DA_HEAD_DIM), 0.1),
        'da_lambda_k2': nrm(ks[12], (L, DA_HEAD_DIM), 0.1),
        'da_subln': gain(ks[13], (L, 2 * DA_HEAD_DIM)),
        'hg_lb_logits': nrm(ks[14], (L + 1, HG_QK), 0.1),
        'hg_norm': gain(ks[15], (L, HG_WIDTH)),
        'w_mem_kv': nrm(ks[16], (L, D_MODEL, 2 * XA_WIDTH), D_MODEL ** -0.5),
        'w_branch_da': nrm(ks[17], (L, DA_WIDTH, D_MODEL), DA_WIDTH ** -0.5),
        'w_branch_hg': nrm(ks[18], (L, HG_WIDTH, D_MODEL), HG_WIDTH ** -0.5),
        'w_branch_xa': nrm(ks[19], (L, XA_WIDTH, D_MODEL), XA_WIDTH ** -0.5),
        'w_out': nrm(ks[20], (L, D_MODEL, D_MODEL), D_MODEL ** -0.5),
        'ffn2_norm': gain(ks[21], (L, D_MODEL)),
        'ffn2_w_gate': nrm(ks[22], (L, D_MODEL, D_FF), D_MODEL ** -0.5),
        'ffn2_w_up': nrm(ks[23], (L, D_MODEL, D_FF), D_MODEL ** -0.5),
        'ffn2_w_down': nrm(ks[24], (L, D_FF, D_MODEL), D_FF ** -0.5),
        'final_norm': gain(ks[25], (D_MODEL,)),
    }


def reference(x, mem, ffn1_norm, ffn1_w_gate, ffn1_w_up, ffn1_w_down, mix_norm, mem_norm, w_in,
              da_lambda_q1, da_lambda_k1, da_lambda_q2, da_lambda_k2, da_subln, hg_lb_logits, hg_norm,
              w_mem_kv, w_branch_da, w_branch_hg, w_branch_xa, w_out,
              ffn2_norm, ffn2_w_gate, ffn2_w_up, ffn2_w_down, final_norm):
    B, S, D = x.shape
    lower_bounds = jnp.cumsum(jax.nn.softmax(hg_lb_logits.astype(jnp.float32), axis=0), axis=0)
    h = x
    for l in range(DEPTH):
        lambda_init = 0.8 - 0.6 * math.exp(-0.3 * l)
        h = h + 0.5 * swiglu(rmsnorm(h, ffn1_norm[l]), ffn1_w_gate[l], ffn1_w_up[l], ffn1_w_down[l])
        u = rmsnorm(h, mix_norm[l])
        proj = u @ w_in[l]
        da_q, da_k, da_v, hg_q, hg_f, hg_i, hg_g, xa_q, gate_logits = split_cols(proj, IN_SIZES)
        y_da = diff_attention(da_q, da_k, da_v, da_lambda_q1[l], da_lambda_k1[l], da_lambda_q2[l],
                              da_lambda_k2[l], da_subln[l], lambda_init)
        y_hg = hgrn2(hg_q, hg_f, hg_i, hg_g, lower_bounds[l], hg_norm[l])
        y_xa = memory_cross_attention(xa_q, rmsnorm(mem, mem_norm[l]), w_mem_kv[l])
        gates = jax.nn.sigmoid(gate_logits.astype(jnp.float32)).astype(h.dtype).reshape(B, S, N_BRANCH, D)
        merged = (gates[:, :, 0] * (y_da @ w_branch_da[l])
                  + gates[:, :, 1] * (y_hg @ w_branch_hg[l])
                  + gates[:, :, 2] * (y_xa @ w_branch_xa[l]))
        h = h + merged @ w_out[l]
        h = h + 0.5 * swiglu(rmsnorm(h, ffn2_norm[l]), ffn2_w_gate[l], ffn2_w_up[l], ffn2_w_down[l])
    return rmsnorm(h, final_norm)
```

```python
import functools
import math

import jax
import jax.numpy as jnp
from jax import lax
from jax.experimental import pallas as pl
from jax.experimental.pallas import tpu as pltpu

F32 = jnp.float32
BF16 = jnp.bfloat16

EPS = 1e-6
DA_HEADS = 4
DA_HEAD_DIM = 128
HG_HEADS = 8
HG_DIM = 128
XA_HEADS = 4
XA_HEAD_DIM = 256
N_BRANCH = 3

V7X_VMEM_BYTES = 64 * 1024 * 1024
V7X_SUBLANES = 8
V7X_LANES = 128

NEG_BIG = -1e30


def _vmem_limit(block_bytes):
    want = int(block_bytes * 1.25) + (4 << 20)
    return min(want, V7X_VMEM_BYTES - (6 << 20))


def _rms(x, g):
    return x * lax.rsqrt(jnp.mean(x * x, axis=-1, keepdims=True) + EPS) * g


def _ffn_kernel(x_ref, g_ref, wg_ref, wu_ref, wd_ref, fg_ref, o_ref, xn_ref, acc_ref, *,
                final_norm):
    j = pl.program_id(1)

    @pl.when(j == 0)
    def _():
        xn_ref[...] = _rms(x_ref[...], g_ref[...]).astype(BF16)
        acc_ref[...] = jnp.zeros_like(acc_ref)

    xn = xn_ref[...]
    a = jnp.dot(xn, wg_ref[...], preferred_element_type=F32)
    b = jnp.dot(xn, wu_ref[...], preferred_element_type=F32)
    hmid = (a * jax.nn.sigmoid(a) * b).astype(BF16)
    acc_ref[...] += jnp.dot(hmid, wd_ref[...], preferred_element_type=F32)

    @pl.when(j == pl.num_programs(1) - 1)
    def _():
        h = x_ref[...] + 0.5 * acc_ref[...]
        if final_norm:
            h = _rms(h, fg_ref[...])
        o_ref[...] = h


def _ffn(x, gain, wg, wu, wd, final_gain, *, final_norm, tm=512, tf=512):
    t, d = x.shape
    dff = wg.shape[1]
    blocks = (2 * tm * d * 4) * 2 + tm * d * 2 + tm * d * 4 + 2 * 3 * d * tf * 2
    return pl.pallas_call(
        functools.partial(_ffn_kernel, final_norm=final_norm),
        grid=(t // tm, dff // tf),
        in_specs=[
            pl.BlockSpec((tm, d), lambda i, j: (i, 0)),
            pl.BlockSpec((1, d), lambda i, j: (0, 0)),
            pl.BlockSpec((d, tf), lambda i, j: (0, j)),
            pl.BlockSpec((d, tf), lambda i, j: (0, j)),
            pl.BlockSpec((tf, d), lambda i, j: (j, 0)),
            pl.BlockSpec((1, d), lambda i, j: (0, 0)),
        ],
        out_specs=pl.BlockSpec((tm, d), lambda i, j: (i, 0)),
        out_shape=jax.ShapeDtypeStruct((t, d), F32),
        scratch_shapes=[pltpu.VMEM((tm, d), BF16), pltpu.VMEM((tm, d), F32)],
        compiler_params=pltpu.CompilerParams(
            dimension_semantics=("parallel", "arbitrary"),
            vmem_limit_bytes=_vmem_limit(blocks)),
        name="ffn_final" if final_norm else "ffn",
    )(x, gain.reshape(1, d), wg, wu, wd, final_gain.reshape(1, d))


def _norm_matmul_kernel(x_ref, g_ref, w_ref, o_ref, xn_ref):
    @pl.when(pl.program_id(1) == 0)
    def _():
        xn_ref[...] = _rms(x_ref[...], g_ref[...]).astype(BF16)

    o_ref[...] = jnp.dot(xn_ref[...], w_ref[...], preferred_element_type=F32).astype(o_ref.dtype)


def _norm_matmul(x, gain, w, out_dtype, *, tm=512, tn=1024, name):
    t, d = x.shape
    n = w.shape[1]
    tm = min(tm, t)
    out_bytes = jnp.dtype(out_dtype).itemsize
    blocks = 2 * tm * d * 4 + tm * d * 2 + 2 * d * tn * 2 + 2 * tm * tn * out_bytes
    return pl.pallas_call(
        _norm_matmul_kernel,
        grid=(t // tm, n // tn),
        in_specs=[
            pl.BlockSpec((tm, d), lambda i, j: (i, 0)),
            pl.BlockSpec((1, d), lambda i, j: (0, 0)),
            pl.BlockSpec((d, tn), lambda i, j: (0, j)),
        ],
        out_specs=pl.BlockSpec((tm, tn), lambda i, j: (i, j)),
        out_shape=jax.ShapeDtypeStruct((t, n), out_dtype),
        scratch_shapes=[pltpu.VMEM((tm, d), BF16)],
        compiler_params=pltpu.CompilerParams(
            dimension_semantics=("parallel", "arbitrary"),
            vmem_limit_bytes=_vmem_limit(blocks)),
        name=name,
    )(x, gain.reshape(1, d), w)


def _diff_attn_kernel(slope_ref, lam_ref, sg_ref, q_ref, k_ref, v_ref, o_ref,
                      m_ref, l_ref, acc_ref, *, tq, tk, lambda_init):
    h = pl.program_id(1)
    qi = pl.program_id(2)
    d = DA_HEAD_DIM
    scale = d ** -0.5
    slope = slope_ref[h]

    m_ref[...] = jnp.full_like(m_ref, NEG_BIG)
    l_ref[...] = jnp.zeros_like(l_ref)
    acc_ref[...] = jnp.zeros_like(acc_ref)

    q = q_ref[0]
    col = lax.broadcasted_iota(jnp.int32, (1, tk), 1)

    def block(ki, masked):
        start = pl.multiple_of(ki * tk, tk)
        kb = k_ref[0, pl.ds(start, tk), :]
        vb = v_ref[0, pl.ds(start, tk), :]
        bias = slope * (col + (ki * tk - qi * tq)).astype(F32)
        if masked:
            row = lax.broadcasted_iota(jnp.int32, (tq, tk), 0)
            keep = row >= lax.broadcasted_iota(jnp.int32, (tq, tk), 1)
        for c in range(2):
            s = lax.dot_general(q[:, c * d:(c + 1) * d], kb[:, c * d:(c + 1) * d],
                                (((1,), (1,)), ((), ())), preferred_element_type=F32)
            s = s * scale + bias
            if masked:
                s = jnp.where(keep, s, NEG_BIG)
            m_old = m_ref[c]
            m_new = jnp.maximum(m_old, jnp.max(s, axis=-1, keepdims=True))
            alpha = jnp.exp(m_old - m_new)
            p = jnp.exp(s - m_new)
            l_ref[c] = alpha * l_ref[c] + jnp.sum(p, axis=-1, keepdims=True)
            acc_ref[c] = alpha * acc_ref[c] + jnp.dot(p.astype(BF16), vb,
                                                      preferred_element_type=F32)
            m_ref[c] = m_new

    def body(ki, carry):
        block(ki, masked=False)
        return carry

    lax.fori_loop(0, qi, body, 0)
    block(qi, masked=True)

    lam4 = lam_ref[...]
    lam = (jnp.exp(jnp.sum(lam4[0:1] * lam4[1:2], axis=-1, keepdims=True))
           - jnp.exp(jnp.sum(lam4[2:3] * lam4[3:4], axis=-1, keepdims=True)) + lambda_init)
    o = acc_ref[0] / l_ref[0] - lam * (acc_ref[1] / l_ref[1])
    o = _rms(o, sg_ref[...]) * (1.0 - lambda_init)
    o_ref[0] = o.astype(o_ref.dtype)


def _diff_attn(proj, slopes, lam4, subln_g, *, lambda_init, tq=512):
    b, s, _ = proj.shape
    hd = 2 * DA_HEAD_DIM
    tk = tq
    blocks = (2 * tq * hd * 2 + 2 * 2 * s * hd * 2 + 2 * tq * hd * 2
              + 2 * tq * hd * 4 + 4 * tq * V7X_LANES * 4 + 6 * tq * tk * 4)
    return pl.pallas_call(
        functools.partial(_diff_attn_kernel, tq=tq, tk=tk, lambda_init=lambda_init),
        grid=(b, DA_HEADS, s // tq),
        in_specs=[
            pl.BlockSpec(memory_space=pltpu.SMEM),
            pl.BlockSpec((4, DA_HEAD_DIM), lambda bi, h, i: (0, 0)),
            pl.BlockSpec((1, hd), lambda bi, h, i: (0, 0)),
            pl.BlockSpec((1, tq, hd), lambda bi, h, i: (bi, i, h)),
            pl.BlockSpec((1, s, hd), lambda bi, h, i: (bi, 0, DA_HEADS + h)),
            pl.BlockSpec((1, s, hd), lambda bi, h, i: (bi, 0, 2 * DA_HEADS + h)),
        ],
        out_specs=pl.BlockSpec((1, tq, hd), lambda bi, h, i: (bi, i, h)),
        out_shape=jax.ShapeDtypeStruct((b, s, DA_HEADS * hd), BF16),
        scratch_shapes=[pltpu.VMEM((2, tq, 1), F32), pltpu.VMEM((2, tq, 1), F32),
                        pltpu.VMEM((2, tq, hd), F32)],
        compiler_params=pltpu.CompilerParams(
            dimension_semantics=("parallel", "parallel", "arbitrary"),
            vmem_limit_bytes=_vmem_limit(blocks)),
        name="diff_attn",
    )(slopes, lam4, subln_g.reshape(1, hd), proj, proj, proj)


def _xattn_kernel(q_ref, k_ref, v_ref, o_ref):
    s = lax.dot_general(q_ref[0], k_ref[0], (((1,), (1,)), ((), ())),
                        preferred_element_type=F32) * (XA_HEAD_DIM ** -0.5)
    p = jnp.exp(s - jnp.max(s, axis=-1, keepdims=True))
    p = p / jnp.sum(p, axis=-1, keepdims=True)
    o_ref[0] = jnp.dot(p.astype(BF16), v_ref[0], preferred_element_type=F32).astype(o_ref.dtype)


def _xattn(proj, kv, *, q_col_block, tq=1024):
    b, s, _ = proj.shape
    m = kv.shape[1]
    hd = XA_HEAD_DIM
    blocks = 2 * 2 * tq * hd * 2 + 2 * 2 * m * hd * 2 + 4 * tq * m * 4
    return pl.pallas_call(
        _xattn_kernel,
        grid=(b, s // tq, XA_HEADS),
        in_specs=[
            pl.BlockSpec((1, tq, hd), lambda bi, i, h: (bi, i, q_col_block + h)),
            pl.BlockSpec((1, m, hd), lambda bi, i, h: (bi, 0, h)),
            pl.BlockSpec((1, m, hd), lambda bi, i, h: (bi, 0, XA_HEADS + h)),
        ],
        out_specs=pl.BlockSpec((1, tq, hd), lambda bi, i, h: (bi, i, h)),
        out_shape=jax.ShapeDtypeStruct((b, s, XA_HEADS * hd), BF16),
        compiler_params=pltpu.CompilerParams(
            dimension_semantics=("parallel", "parallel", "parallel"),
            vmem_limit_bytes=_vmem_limit(blocks)),
        name="mem_xattn",
    )(proj, kv, kv)


HG_CHUNK = 128
HG_BASE = V7X_SUBLANES


def _hgrn_kernel(lbl_ref, ng_ref, q_ref, f_ref, i_ref, g_ref, o_ref, st_ref, *, ts, layer):
    c = HG_CHUNK
    kd = HG_DIM

    @pl.when(pl.program_id(2) == 0)
    def _():
        st_ref[...] = jnp.zeros_like(st_ref)

    logits = [lbl_ref[r, 0] for r in range(lbl_ref.shape[0])]
    mx = functools.reduce(jnp.maximum, logits)
    ex = [jnp.exp(v - mx) for v in logits]
    lb = functools.reduce(lambda u, v: u + v, ex[:layer + 1]) / functools.reduce(
        lambda u, v: u + v, ex)
    ng = ng_ref[0]

    row = lax.broadcasted_iota(jnp.int32, (c, c), 0)
    colm = lax.broadcasted_iota(jnp.int32, (c, c), 1)
    tri = (row >= colm).astype(F32)
    rcol = lax.broadcasted_iota(jnp.int32, (c, 1), 0)
    levels = []
    m = HG_BASE
    while m < c:
        levels.append(m)
        m *= 2

    def chunk(ci, carry):
        sl = pl.ds(pl.multiple_of(ci * c, c), c)
        z = f_ref[0, sl, :]
        qq = q_ref[0, sl, :]
        v = i_ref[0, sl, :]
        og = g_ref[0, sl, :]

        logf = jnp.log(lb + (1.0 - lb) * jax.nn.sigmoid(z))
        kk = (1.0 - lb) * jax.nn.sigmoid(-z)
        qf = qq * jax.nn.sigmoid(qq)
        b = jnp.dot(tri, logf, preferred_element_type=F32, precision=lax.Precision.HIGHEST)
        v16 = v.astype(BF16)

        o = jnp.zeros((c, kd), F32)
        for dlt in range(HG_BASE):
            if dlt == 0:
                w = jnp.sum(qf * kk, axis=-1, keepdims=True)
                o = o + w * v
            else:
                ok = (rcol % HG_BASE) >= dlt
                e = jnp.where(ok, b - pltpu.roll(b, dlt, 0), 0.0)
                w = jnp.sum(qf * pltpu.roll(kk, dlt, 0) * jnp.exp(e), axis=-1, keepdims=True)
                o = o + jnp.where(ok, w, 0.0) * pltpu.roll(v, dlt, 0)

        a_tot = jnp.zeros((c, c), F32)
        for m in levels:
            grp = c // (2 * m)
            b_edge = jnp.broadcast_to(b.reshape(grp, 2 * m, kd)[:, m - 1:m, :],
                                      (grp, 2 * m, kd)).reshape(c, kd)
            qt = qf * jnp.exp(jnp.minimum(b - b_edge, 0.0))
            kt = kk * jnp.exp(jnp.minimum(b_edge - b, 0.0))
            a_m = lax.dot_general(qt.astype(BF16), kt.astype(BF16), (((1,), (1,)), ((), ())),
                                  preferred_element_type=F32)
            keep = ((row // (2 * m)) == (colm // (2 * m))) & ((row % (2 * m)) >= m) & (
                (colm % (2 * m)) < m)
            a_tot = a_tot + jnp.where(keep, a_m, 0.0)
        o = o + jnp.dot(a_tot.astype(BF16), v16, preferred_element_type=F32)

        st = st_ref[...]
        o = o + lax.dot_general((qf * jnp.exp(b)).astype(BF16), st.astype(BF16),
                                (((1,), (1,)), ((), ())), preferred_element_type=F32)
        b_end = b[c - 1:c, :]
        kbar = (kk * jnp.exp(b_end - b)).astype(BF16)
        st_ref[...] = st * jnp.exp(b_end) + lax.dot_general(
            v16, kbar, (((0,), (0,)), ((), ())), preferred_element_type=F32)

        o = o * lax.rsqrt(jnp.mean(o * o, axis=-1, keepdims=True) + EPS) * ng
        o = o * jax.nn.sigmoid(og)
        o_ref[0, sl, :] = o.astype(o_ref.dtype)
        return carry

    lax.fori_loop(0, ts // c, chunk, 0)


def _hgrn(proj, lb_logits, norm_g, *, layer, ts=512):
    b, s, _ = proj.shape
    kd = HG_DIM
    nl = lb_logits.shape[0]
    blocks = 2 * 4 * ts * kd * 4 + 2 * ts * kd * 2 + kd * kd * 4 + 64 * HG_CHUNK * kd * 4

    def col(off):
        return lambda bi, h, i: (bi, i, off * HG_HEADS + h)

    return pl.pallas_call(
        functools.partial(_hgrn_kernel, ts=ts, layer=layer),
        grid=(b, HG_HEADS, s // ts),
        in_specs=[
            pl.BlockSpec((nl, 1, 1, kd), lambda bi, h, i: (0, h, 0, 0)),
            pl.BlockSpec((1, 1, kd), lambda bi, h, i: (h, 0, 0)),
            pl.BlockSpec((1, ts, kd), col(0)),
            pl.BlockSpec((1, ts, kd), col(1)),
            pl.BlockSpec((1, ts, kd), col(2)),
            pl.BlockSpec((1, ts, kd), col(3)),
        ],
        out_specs=pl.BlockSpec((1, ts, kd), lambda bi, h, i: (bi, i, h)),
        out_shape=jax.ShapeDtypeStruct((b, s, HG_HEADS * kd), BF16),
        scratch_shapes=[pltpu.VMEM((kd, kd), F32)],
        compiler_params=pltpu.CompilerParams(
            dimension_semantics=("parallel", "parallel", "arbitrary"),
            vmem_limit_bytes=_vmem_limit(blocks)),
        name="hgrn2",
    )(lb_logits.reshape(nl, HG_HEADS, 1, kd), norm_g.reshape(HG_HEADS, 1, kd),
      proj, proj, proj, proj)


def _merge_kernel(h_ref, g_ref, y0_ref, y1_ref, y2_ref, wgate_ref, wbr_ref, wout_ref, o_ref,
                  u_ref, acc_ref):
    j = pl.program_id(1)

    @pl.when(j == 0)
    def _():
        u_ref[...] = _rms(h_ref[...], g_ref[...]).astype(BF16)
        acc_ref[...] = jnp.zeros_like(acc_ref)

    u = u_ref[...]
    merged = None
    for br, y_ref in enumerate((y0_ref, y1_ref, y2_ref)):
        gate = jax.nn.sigmoid(jnp.dot(u, wgate_ref[br], preferred_element_type=F32))
        term = gate * jnp.dot(y_ref[...], wbr_ref[br], preferred_element_type=F32)
        merged = term if merged is None else merged + term
    acc_ref[...] += jnp.dot(merged.astype(BF16), wout_ref[...], preferred_element_type=F32)

    @pl.when(j == pl.num_programs(1) - 1)
    def _():
        o_ref[...] = h_ref[...] + acc_ref[...]


def _merge(h, gain, y0, y1, y2, wgate, wbr, wout, *, tm=512, tn=256):
    t, d = h.shape
    w = y0.shape[1]
    blocks = (2 * tm * d * 4 * 2 + tm * d * 2 + tm * d * 4 + 2 * 3 * tm * w * 2
              + 2 * 3 * d * tn * 2 + 2 * 3 * w * tn * 2 + 2 * tn * d * 2 + 8 * tm * tn * 4)
    return pl.pallas_call(
        _merge_kernel,
        grid=(t // tm, d // tn),
        in_specs=[
            pl.BlockSpec((tm, d), lambda i, j: (i, 0)),
            pl.BlockSpec((1, d), lambda i, j: (0, 0)),
            pl.BlockSpec((tm, w), lambda i, j: (i, 0)),
            pl.BlockSpec((tm, w), lambda i, j: (i, 0)),
            pl.BlockSpec((tm, w), lambda i, j: (i, 0)),
            pl.BlockSpec((N_BRANCH, d, tn), lambda i, j: (0, 0, j)),
            pl.BlockSpec((N_BRANCH, w, tn), lambda i, j: (0, 0, j)),
            pl.BlockSpec((tn, d), lambda i, j: (j, 0)),
        ],
        out_specs=pl.BlockSpec((tm, d), lambda i, j: (i, 0)),
        out_shape=jax.ShapeDtypeStruct((t, d), F32),
        scratch_shapes=[pltpu.VMEM((tm, d), BF16), pltpu.VMEM((tm, d), F32)],
        compiler_params=pltpu.CompilerParams(
            dimension_semantics=("parallel", "arbitrary"),
            vmem_limit_bytes=_vmem_limit(blocks)),
        name="merge_out",
    )(h, gain.reshape(1, d), y0, y1, y2, wgate, wbr, wout)


def kernel(x, mem, ffn1_norm, ffn1_w_gate, ffn1_w_up, ffn1_w_down, mix_norm, mem_norm, w_in,
           da_lambda_q1, da_lambda_k1, da_lambda_q2, da_lambda_k2, da_subln, hg_lb_logits, hg_norm,
           w_mem_kv, w_branch_da, w_branch_hg, w_branch_xa, w_out,
           ffn2_norm, ffn2_w_gate, ffn2_w_up, ffn2_w_down, final_norm):
    bsz, seq, d = x.shape
    depth = w_in.shape[0]
    t = bsz * seq
    da_w = DA_HEADS * 2 * DA_HEAD_DIM
    hg_w = HG_HEADS * HG_DIM
    xa_w = XA_HEADS * XA_HEAD_DIM
    slopes = jnp.asarray([2.0 ** (-8.0 * (i + 1) / DA_HEADS) for i in range(DA_HEADS)], F32)

    h = x.reshape(t, d)
    for l in range(depth):
        lambda_init = 0.8 - 0.6 * math.exp(-0.3 * l)
        last = l == depth - 1
        h = _ffn(h, ffn1_norm[l], ffn1_w_gate[l].astype(BF16), ffn1_w_up[l].astype(BF16),
                 ffn1_w_down[l].astype(BF16), final_norm, final_norm=False)

        wi = w_in[l]
        hg_lo = 3 * da_w
        hg_hi = hg_lo + 4 * hg_w
        w_att = jnp.concatenate([wi[:, :hg_lo], wi[:, hg_hi:hg_hi + xa_w]], axis=1).astype(BF16)
        w_hg = wi[:, hg_lo:hg_hi].astype(BF16)
        gate_lo = hg_hi + xa_w
        w_gate = wi[:, gate_lo:].reshape(d, N_BRANCH, d).transpose(1, 0, 2).astype(BF16)
        p_att = _norm_matmul(h, mix_norm[l], w_att, BF16, name="in_proj_att")
        p_hg = _norm_matmul(h, mix_norm[l], w_hg, F32, name="in_proj_hg")
        p_att = p_att.reshape(bsz, seq, -1)
        p_hg = p_hg.reshape(bsz, seq, -1)

        lam4 = jnp.stack([da_lambda_q1[l], da_lambda_k1[l], da_lambda_q2[l], da_lambda_k2[l]])
        y_da = _diff_attn(p_att, slopes, lam4, da_subln[l], lambda_init=lambda_init)
        y_hg = _hgrn(p_hg, hg_lb_logits, hg_norm[l], layer=l)
        kv = _norm_matmul(mem.reshape(-1, d), mem_norm[l], w_mem_kv[l].astype(BF16), BF16,
                          name="mem_kv").reshape(bsz, mem.shape[1], -1)
        y_xa = _xattn(p_att, kv, q_col_block=3 * da_w // XA_HEAD_DIM)

        w_br = jnp.stack([w_branch_da[l], w_branch_hg[l], w_branch_xa[l]]).astype(BF16)
        h = _merge(h, mix_norm[l], y_da.reshape(t, da_w), y_hg.reshape(t, hg_w),
                   y_xa.reshape(t, xa_w), w_gate, w_br, w_out[l].astype(BF16))

        h = _ffn(h, ffn2_norm[l], ffn2_w_gate[l].astype(BF16), ffn2_w_up[l].astype(BF16),
                 ffn2_w_down[l].astype(BF16), final_norm, final_norm=last)
    return h.reshape(bsz, seq, d)
```

```python
import functools
import math

import numpy as np
import jax
import jax.numpy as jnp
from jax import lax
from jax.experimental import pallas as pl
from jax.experimental.pallas import tpu as pltpu

F32 = jnp.float32
BF16 = jnp.bfloat16

EPS = 1e-6
DA_HEADS = 4
DA_HEAD_DIM = 128
HG_HEADS = 8
HG_DIM = 128
XA_HEADS = 4
XA_HEAD_DIM = 256
N_BRANCH = 3
LOG2E = math.log2(math.e)

V7X_VMEM_BYTES = 64 * 1024 * 1024
V7X_LANES = 128

NEG_BIG = -1e30


def _vmem_limit(block_bytes):
    want = int(block_bytes * 1.25) + (4 << 20)
    return min(want, V7X_VMEM_BYTES - (6 << 20))


def _rms(x, g):
    return x * lax.rsqrt(jnp.mean(x * x, axis=-1, keepdims=True) + EPS) * g


def _dot_nt(a, b):
    return lax.dot_general(a, b, (((1,), (1,)), ((), ())), preferred_element_type=F32)


def _ffn_kernel(x_ref, g_ref, wg_ref, wu_ref, wd_ref, fg_ref, o_ref, xn_ref, acc_ref, *,
                final_norm):
    j = pl.program_id(1)

    @pl.when(j == 0)
    def _():
        xn_ref[...] = _rms(x_ref[...], g_ref[...]).astype(BF16)
        acc_ref[...] = jnp.zeros_like(acc_ref)

    xn = xn_ref[...]
    a = jnp.dot(xn, wg_ref[...], preferred_element_type=F32)
    b = jnp.dot(xn, wu_ref[...], preferred_element_type=F32)
    hmid = (a * jax.nn.sigmoid(a) * b).astype(BF16)
    acc_ref[...] += jnp.dot(hmid, wd_ref[...], preferred_element_type=F32)

    @pl.when(j == pl.num_programs(1) - 1)
    def _():
        h = x_ref[...] + 0.5 * acc_ref[...]
        if final_norm:
            h = _rms(h, fg_ref[...])
        o_ref[...] = h


def _ffn(x, gain, wg, wu, wd, final_gain, *, final_norm, tm=512, tf=512):
    t, d = x.shape
    dff = wg.shape[1]
    blocks = (2 * tm * d * 4) * 2 + tm * d * 2 + tm * d * 4 + 2 * 3 * d * tf * 2
    return pl.pallas_call(
        functools.partial(_ffn_kernel, final_norm=final_norm),
        grid=(t // tm, dff // tf),
        in_specs=[
            pl.BlockSpec((tm, d), lambda i, j: (i, 0)),
            pl.BlockSpec((1, d), lambda i, j: (0, 0)),
            pl.BlockSpec((d, tf), lambda i, j: (0, j)),
            pl.BlockSpec((d, tf), lambda i, j: (0, j)),
            pl.BlockSpec((tf, d), lambda i, j: (j, 0)),
            pl.BlockSpec((1, d), lambda i, j: (0, 0)),
        ],
        out_specs=pl.BlockSpec((tm, d), lambda i, j: (i, 0)),
        out_shape=jax.ShapeDtypeStruct((t, d), F32),
        scratch_shapes=[pltpu.VMEM((tm, d), BF16), pltpu.VMEM((tm, d), F32)],
        compiler_params=pltpu.CompilerParams(
            dimension_semantics=("parallel", "arbitrary"),
            vmem_limit_bytes=_vmem_limit(blocks)),
        name="ffn_final" if final_norm else "ffn",
    )(x, gain.reshape(1, d), wg, wu, wd, final_gain.reshape(1, d))


def _norm_matmul_kernel(x_ref, g_ref, w_ref, o_ref, xn_ref):
    @pl.when(pl.program_id(1) == 0)
    def _():
        xn_ref[...] = _rms(x_ref[...], g_ref[...]).astype(BF16)

    o_ref[...] = jnp.dot(xn_ref[...], w_ref[...], preferred_element_type=F32).astype(o_ref.dtype)


def _norm_matmul(x, gain, w, out_dtype, *, tm=512, tn=1024, name):
    t, d = x.shape
    n = w.shape[1]
    out_bytes = jnp.dtype(out_dtype).itemsize
    blocks = 2 * tm * d * 4 + tm * d * 2 + 2 * d * tn * 2 + 2 * tm * tn * out_bytes
    return pl.pallas_call(
        _norm_matmul_kernel,
        grid=(t // tm, n // tn),
        in_specs=[
            pl.BlockSpec((tm, d), lambda i, j: (i, 0)),
            pl.BlockSpec((1, d), lambda i, j: (0, 0)),
            pl.BlockSpec((d, tn), lambda i, j: (0, j)),
        ],
        out_specs=pl.BlockSpec((tm, tn), lambda i, j: (i, j)),
        out_shape=jax.ShapeDtypeStruct((t, n), out_dtype),
        scratch_shapes=[pltpu.VMEM((tm, d), BF16)],
        compiler_params=pltpu.CompilerParams(
            dimension_semantics=("parallel", "arbitrary"),
            vmem_limit_bytes=_vmem_limit(blocks)),
        name=name,
    )(x, gain.reshape(1, d), w)


def _in_proj_kernel(x_ref, g_ref, w_ref, att_ref, hg_ref, xn_ref, *, n_att):
    j = pl.program_id(1)

    @pl.when(j == 0)
    def _():
        xn_ref[...] = _rms(x_ref[...], g_ref[...]).astype(BF16)

    res = jnp.dot(xn_ref[...], w_ref[...], preferred_element_type=F32)

    @pl.when(j < n_att)
    def _():
        att_ref[...] = res.astype(att_ref.dtype)

    @pl.when(j >= n_att)
    def _():
        hg_ref[...] = res


def _in_proj(x, gain, w_in, *, da_cols, hg_cols, xa_cols, tm=512, tn=1024):
    t, d = x.shape
    da_b, hg_b, xa_b = da_cols // tn, hg_cols // tn, xa_cols // tn
    n_att = da_b + xa_b

    def w_col(j):
        return jnp.where(j < da_b, j, jnp.where(j < n_att, j + hg_b, j - xa_b))

    blocks = (2 * tm * d * 4 + tm * d * 2 + 2 * d * tn * 2 + 2 * tm * tn * 2 + 2 * tm * tn * 4
              + tm * tn * 4)
    return pl.pallas_call(
        functools.partial(_in_proj_kernel, n_att=n_att),
        grid=(t // tm, n_att + hg_b),
        in_specs=[
            pl.BlockSpec((tm, d), lambda i, j: (i, 0)),
            pl.BlockSpec((1, d), lambda i, j: (0, 0)),
            pl.BlockSpec((d, tn), lambda i, j: (0, w_col(j))),
        ],
        out_specs=[
            pl.BlockSpec((tm, tn), lambda i, j: (i, jnp.minimum(j, n_att - 1))),
            pl.BlockSpec((tm, tn), lambda i, j: (i, jnp.maximum(j - n_att, 0))),
        ],
        out_shape=[jax.ShapeDtypeStruct((t, da_cols + xa_cols), BF16),
                   jax.ShapeDtypeStruct((t, hg_cols), F32)],
        scratch_shapes=[pltpu.VMEM((tm, d), BF16)],
        compiler_params=pltpu.CompilerParams(
            dimension_semantics=("parallel", "arbitrary"),
            vmem_limit_bytes=_vmem_limit(blocks)),
        name="in_proj",
    )(x, gain.reshape(1, d), w_in)


def _diff_attn_kernel(slope_ref, lam_ref, sg_ref, q_ref, k_ref, v_ref, o_ref,
                      m_ref, l_ref, acc_ref, *, tq, tr, tk, lambda_init):
    h = pl.program_id(1)
    qi = pl.program_id(2)
    d = DA_HEAD_DIM
    hd = 2 * d
    nr = tq // tr
    scale2 = d ** -0.5 * LOG2E
    slope2 = slope_ref[h] * LOG2E

    m_ref[...] = jnp.full_like(m_ref, NEG_BIG)
    l_ref[...] = jnp.zeros_like(l_ref)
    acc_ref[...] = jnp.zeros_like(acc_ref)

    def lanes(x, width):
        return jnp.concatenate([x] * (width // V7X_LANES), axis=1)

    def tiles(row_tiles, k_start, width, masked_r):
        kb = k_ref[0, pl.ds(k_start, width), :]
        vb = v_ref[0, pl.ds(k_start, width), :]
        col = lax.broadcasted_iota(jnp.int32, (1, width), 1)
        chains = [(r, c) for r in row_tiles for c in range(2)]
        s_all = {}
        for r, c in chains:
            bias = slope2 * (col + (k_start - (qi * tq + r * tr))).astype(F32)
            q = q_ref[0, r * tr:(r + 1) * tr, c * d:(c + 1) * d]
            s = _dot_nt(q, kb[:, c * d:(c + 1) * d]) * scale2 + bias
            if r == masked_r:
                keep = (lax.broadcasted_iota(jnp.int32, (tr, width), 0)
                        >= lax.broadcasted_iota(jnp.int32, (tr, width), 1))
                s = jnp.where(keep, s, NEG_BIG)
            s_all[r, c] = s
        p_all, alpha_all = {}, {}
        for r, c in chains:
            idx = 2 * r + c
            s = s_all[r, c]
            m_old = m_ref[idx]
            m_new = jnp.maximum(m_old, jnp.max(s, axis=-1, keepdims=True))
            alpha = jnp.exp2(m_old - m_new)
            p = jnp.exp2(s - lanes(m_new, width))
            l_ref[idx] = alpha * l_ref[idx] + jnp.sum(p, axis=-1, keepdims=True)
            m_ref[idx] = m_new
            p_all[r, c] = p.astype(BF16)
            alpha_all[r, c] = alpha
        for r, c in chains:
            idx = 2 * r + c
            acc_ref[idx] = lanes(alpha_all[r, c], hd) * acc_ref[idx] + jnp.dot(
                p_all[r, c], vb, preferred_element_type=F32)

    def body(ki, carry):
        tiles(range(nr), pl.multiple_of(ki * tk, tk), tk, None)
        return carry

    lax.fori_loop(0, qi * (tq // tk), body, 0)
    for cc in range(nr):
        tiles(range(cc, nr), pl.multiple_of(qi * tq + cc * tr, tr), tr, cc)

    lam4 = lam_ref[...]
    lam = (jnp.exp(jnp.sum(lam4[0:1] * lam4[1:2], axis=-1, keepdims=True))
           - jnp.exp(jnp.sum(lam4[2:3] * lam4[3:4], axis=-1, keepdims=True)) + lambda_init)
    for r in range(nr):
        o = (acc_ref[2 * r] / lanes(l_ref[2 * r], hd)
             - lam * (acc_ref[2 * r + 1] / lanes(l_ref[2 * r + 1], hd)))
        o = _rms(o, sg_ref[...]) * (1.0 - lambda_init)
        o_ref[0, r * tr:(r + 1) * tr, :] = o.astype(o_ref.dtype)


def _diff_attn(proj, slopes, lam4, subln_g, *, lambda_init, tq=1024, tr=512, tk=512):
    b, s, _ = proj.shape
    hd = 2 * DA_HEAD_DIM
    nchain = 2 * (tq // tr)
    blocks = (2 * tq * hd * 2 + 2 * 2 * s * hd * 2 + 2 * tq * hd * 2
              + nchain * tr * (hd + 2 * V7X_LANES) * 4 + 4 * nchain * tr * tk * 4)
    return pl.pallas_call(
        functools.partial(_diff_attn_kernel, tq=tq, tr=tr, tk=tk, lambda_init=lambda_init),
        grid=(b, DA_HEADS, s // tq),
        in_specs=[
            pl.BlockSpec(memory_space=pltpu.SMEM),
            pl.BlockSpec((4, DA_HEAD_DIM), lambda bi, h, i: (0, 0)),
            pl.BlockSpec((1, hd), lambda bi, h, i: (0, 0)),
            pl.BlockSpec((1, tq, hd), lambda bi, h, i: (bi, i, h)),
            pl.BlockSpec((1, s, hd), lambda bi, h, i: (bi, 0, DA_HEADS + h)),
            pl.BlockSpec((1, s, hd), lambda bi, h, i: (bi, 0, 2 * DA_HEADS + h)),
        ],
        out_specs=pl.BlockSpec((1, tq, hd), lambda bi, h, i: (bi, i, h)),
        out_shape=jax.ShapeDtypeStruct((b, s, DA_HEADS * hd), BF16),
        scratch_shapes=[pltpu.VMEM((nchain, tr, V7X_LANES), F32),
                        pltpu.VMEM((nchain, tr, V7X_LANES), F32),
                        pltpu.VMEM((nchain, tr, hd), F32)],
        compiler_params=pltpu.CompilerParams(
            dimension_semantics=("parallel", "parallel", "arbitrary"),
            vmem_limit_bytes=_vmem_limit(blocks)),
        name="diff_attn",
    )(slopes, lam4, subln_g.reshape(1, hd), proj, proj, proj)


def _xattn_kernel(q_ref, k_ref, v_ref, o_ref):
    s = _dot_nt(q_ref[0], k_ref[0]) * (XA_HEAD_DIM ** -0.5)
    p = jnp.exp(s - jnp.max(s, axis=-1, keepdims=True))
    p = p / jnp.sum(p, axis=-1, keepdims=True)
    o_ref[0] = jnp.dot(p.astype(BF16), v_ref[0], preferred_element_type=F32).astype(o_ref.dtype)


def _xattn(proj, kv, *, q_col_block, tq=1024):
    b, s, _ = proj.shape
    m = kv.shape[1]
    hd = XA_HEAD_DIM
    blocks = 2 * 2 * tq * hd * 2 + 2 * 2 * m * hd * 2 + 4 * tq * m * 4
    return pl.pallas_call(
        _xattn_kernel,
        grid=(b, s // tq, XA_HEADS),
        in_specs=[
            pl.BlockSpec((1, tq, hd), lambda bi, i, h: (bi, i, q_col_block + h)),
            pl.BlockSpec((1, m, hd), lambda bi, i, h: (bi, 0, h)),
            pl.BlockSpec((1, m, hd), lambda bi, i, h: (bi, 0, XA_HEADS + h)),
        ],
        out_specs=pl.BlockSpec((1, tq, hd), lambda bi, i, h: (bi, i, h)),
        out_shape=jax.ShapeDtypeStruct((b, s, XA_HEADS * hd), BF16),
        compiler_params=pltpu.CompilerParams(
            dimension_semantics=("parallel", "parallel", "parallel"),
            vmem_limit_bytes=_vmem_limit(blocks)),
        name="mem_xattn",
    )(proj, kv, kv)


HG_CHUNK = 128


def _hgrn_levels(c):
    levels, m = [], 1
    while m < c:
        levels.append(m)
        m *= 2
    return levels


def _hgrn_constants(c):
    t = np.arange(c)[:, None]
    s = np.arange(c)[None, :]
    intervals, masks = [], []
    for m in _hgrn_levels(c):
        edge = (t // (2 * m)) * (2 * m) + m - 1
        upper = (t % (2 * m)) >= m
        intervals.append(np.where(upper, (s > edge) & (s <= t), (s > t) & (s <= edge)))
        masks.append(upper & ((s // (2 * m)) == (t // (2 * m))) & ((s % (2 * m)) < m))
    intervals.append(s <= t)
    intervals.append(s > t)
    iv = np.concatenate(intervals, 0)
    return (jnp.asarray(np.concatenate([iv, iv], 1), BF16),
            jnp.asarray(np.concatenate(masks, 0), F32))


def _hgrn_kernel(lbl_ref, ng_ref, iv_ref, mk_ref, q_ref, f_ref, i_ref, g_ref, o_ref, st_ref, *,
                 ts, nh, layer):
    c = HG_CHUNK
    kd = HG_DIM
    levels = _hgrn_levels(c)
    nl = len(levels)

    @pl.when(pl.program_id(2) == 0)
    def _():
        st_ref[...] = jnp.zeros_like(st_ref)

    logits = [lbl_ref[r, 0] for r in range(lbl_ref.shape[0])]
    mx = functools.reduce(jnp.maximum, logits)
    ex = [jnp.exp(v - mx) for v in logits]
    lb = sum(ex[1:layer + 1], ex[0]) / sum(ex[1:], ex[0])
    ng = ng_ref[0]

    def head(x, hd):
        return x[:, hd * kd:(hd + 1) * kd]

    nc = ts // c
    rows = [slice(ci * c, (ci + 1) * c) for ci in range(nc)]

    qf, kk, v16, dec, a_tot = {}, {}, {}, {}, {}

    def stage1(ci):
        sl = rows[ci]
        z = f_ref[0, sl, :]
        qq = q_ref[0, sl, :]
        sig = jax.nn.sigmoid(z)
        lf2 = jnp.log(lb + (1.0 - lb) * sig) * LOG2E
        kk[ci] = (1.0 - lb) * (1.0 - sig)
        qf[ci] = qq * jax.nn.sigmoid(qq)
        v16[ci] = i_ref[0, sl, :].astype(BF16)
        hi = lf2.astype(BF16)
        lo = (lf2 - hi.astype(F32)).astype(BF16)
        parts = jnp.concatenate([hi, lo], axis=0)
        dec[ci] = [jnp.exp2(jnp.dot(iv_ref[p * c:(p + 1) * c, :], parts,
                                    preferred_element_type=F32)) for p in range(nl + 2)]

    def stage2(ci):
        acc = [jnp.zeros((c, c), F32) for _ in range(nh)]
        for li in range(nl):
            qt = (qf[ci] * dec[ci][li]).astype(BF16)
            kt = (kk[ci] * dec[ci][li]).astype(BF16)
            mask = mk_ref[li * c:(li + 1) * c, :]
            for hd in range(nh):
                acc[hd] = acc[hd] + _dot_nt(head(qt, hd), head(kt, hd)) * mask
        a_tot[ci] = [a.astype(BF16) for a in acc]

    st = [st_ref[hd] for hd in range(nh)]

    def stage3(ci):
        sl = rows[ci]
        e_b = dec[ci][nl]
        e_r = dec[ci][nl + 1]
        qb = (qf[ci] * e_b).astype(BF16)
        kr = (kk[ci] * e_r).astype(BF16)
        qk = qf[ci] * kk[ci]
        v = i_ref[0, sl, :]
        outs = []
        for hd in range(nh):
            vh = head(v16[ci], hd)
            o = jnp.sum(head(qk, hd), axis=-1, keepdims=True) * head(v, hd)
            o = o + jnp.dot(a_tot[ci][hd], vh, preferred_element_type=F32)
            o = o + _dot_nt(head(qb, hd), st[hd].astype(BF16))
            st[hd] = st[hd] * head(e_b, hd)[c - 1:c, :] + lax.dot_general(
                vh, head(kr, hd), (((0,), (0,)), ((), ())), preferred_element_type=F32)
            outs.append(o * lax.rsqrt(jnp.mean(o * o, axis=-1, keepdims=True) + EPS))
        o = jnp.concatenate(outs, axis=1) * ng * jax.nn.sigmoid(g_ref[0, sl, :])
        o_ref[0, sl, :] = o.astype(o_ref.dtype)

    for step in range(nc + 2):
        if step < nc:
            stage1(step)
        if 0 <= step - 1 < nc:
            stage2(step - 1)
        if 0 <= step - 2 < nc:
            stage3(step - 2)
    for hd in range(nh):
        st_ref[hd] = st[hd]


def _hgrn(proj, lb_logits, norm_g, *, layer, ts=1024, nh=2):
    b, s, _ = proj.shape
    kd = HG_DIM
    c = HG_CHUNK
    w = nh * kd
    ng = HG_HEADS // nh
    nl = lb_logits.shape[0]
    intervals, masks = _hgrn_constants(c)
    blocks = (2 * 4 * ts * w * 4 + 2 * ts * w * 2 + nh * kd * kd * 4 + 2 * intervals.size * 2
              + 2 * masks.size * 4 + 96 * c * w * 4)

    def col(off):
        return lambda bi, h, i: (bi, i, off * ng + h)

    return pl.pallas_call(
        functools.partial(_hgrn_kernel, ts=ts, nh=nh, layer=layer),
        grid=(b, ng, s // ts),
        in_specs=[
            pl.BlockSpec((nl, 1, 1, w), lambda bi, h, i: (0, h, 0, 0)),
            pl.BlockSpec((1, 1, w), lambda bi, h, i: (h, 0, 0)),
            pl.BlockSpec(intervals.shape, lambda bi, h, i: (0, 0)),
            pl.BlockSpec(masks.shape, lambda bi, h, i: (0, 0)),
            pl.BlockSpec((1, ts, w), col(0)),
            pl.BlockSpec((1, ts, w), col(1)),
            pl.BlockSpec((1, ts, w), col(2)),
            pl.BlockSpec((1, ts, w), col(3)),
        ],
        out_specs=pl.BlockSpec((1, ts, w), lambda bi, h, i: (bi, i, h)),
        out_shape=jax.ShapeDtypeStruct((b, s, HG_HEADS * kd), BF16),
        scratch_shapes=[pltpu.VMEM((nh, kd, kd), F32)],
        compiler_params=pltpu.CompilerParams(
            dimension_semantics=("parallel", "parallel", "arbitrary"),
            vmem_limit_bytes=_vmem_limit(blocks)),
        name="hgrn2",
    )(lb_logits.reshape(nl, ng, 1, w), norm_g.reshape(ng, 1, w), intervals, masks,
      proj, proj, proj, proj)


def _merge_kernel(h_ref, g_ref, y0_ref, y1_ref, y2_ref, wg0_ref, wg1_ref, wg2_ref,
                  wb0_ref, wb1_ref, wb2_ref, wout_ref, o_ref, u_ref, mg_ref):
    j = pl.program_id(1)
    nj = mg_ref.shape[0]
    tn = mg_ref.shape[2]

    @pl.when(j == 0)
    def _():
        u_ref[...] = _rms(h_ref[...], g_ref[...]).astype(BF16)

    u = u_ref[...]
    merged = None
    for y_ref, wg_ref, wb_ref in ((y0_ref, wg0_ref, wb0_ref), (y1_ref, wg1_ref, wb1_ref),
                                  (y2_ref, wg2_ref, wb2_ref)):
        gate = jax.nn.sigmoid(jnp.dot(u, wg_ref[...], preferred_element_type=F32))
        term = gate * jnp.dot(y_ref[...], wb_ref[...], preferred_element_type=F32)
        merged = term if merged is None else merged + term
    mg_ref[j] = merged.astype(BF16)

    @pl.when(j == nj - 1)
    def _():
        acc = h_ref[...]
        for jj in range(nj):
            acc = acc + jnp.dot(mg_ref[jj], wout_ref[jj * tn:(jj + 1) * tn, :],
                                preferred_element_type=F32)
        o_ref[...] = acc


def _merge(h, gain, y0, y1, y2, w_in, gate_col0, wb0, wb1, wb2, wout, *, tm=512, tn=256):
    t, d = h.shape
    w = y0.shape[1]
    nj = d // tn
    blocks = (2 * tm * d * 4 * 2 + tm * d * 2 + tm * d * 2 + 2 * 3 * tm * w * 2
              + 2 * 3 * d * tn * 2 + 2 * 3 * w * tn * 2 + d * d * 2 + 8 * tm * tn * 4)

    def gate_spec(br):
        off = (gate_col0 + br * d) // tn
        return pl.BlockSpec((d, tn), lambda i, j: (0, off + j))

    y_spec = pl.BlockSpec((tm, w), lambda i, j: (i, 0))
    wb_spec = pl.BlockSpec((w, tn), lambda i, j: (0, j))
    return pl.pallas_call(
        _merge_kernel,
        grid=(t // tm, nj),
        in_specs=[
            pl.BlockSpec((tm, d), lambda i, j: (i, 0)),
            pl.BlockSpec((1, d), lambda i, j: (0, 0)),
            y_spec, y_spec, y_spec,
            gate_spec(0), gate_spec(1), gate_spec(2),
            wb_spec, wb_spec, wb_spec,
            pl.BlockSpec((d, d), lambda i, j: (0, 0), pipeline_mode=pl.Buffered(1)),
        ],
        out_specs=pl.BlockSpec((tm, d), lambda i, j: (i, 0)),
        out_shape=jax.ShapeDtypeStruct((t, d), F32),
        scratch_shapes=[pltpu.VMEM((tm, d), BF16), pltpu.VMEM((nj, tm, tn), BF16)],
        compiler_params=pltpu.CompilerParams(
            dimension_semantics=("parallel", "arbitrary"),
            vmem_limit_bytes=_vmem_limit(blocks)),
        name="merge_out",
    )(h, gain.reshape(1, d), y0, y1, y2, w_in, w_in, w_in, wb0, wb1, wb2, wout)


def kernel(x, mem, ffn1_norm, ffn1_w_gate, ffn1_w_up, ffn1_w_down, mix_norm, mem_norm, w_in,
           da_lambda_q1, da_lambda_k1, da_lambda_q2, da_lambda_k2, da_subln, hg_lb_logits, hg_norm,
           w_mem_kv, w_branch_da, w_branch_hg, w_branch_xa, w_out,
           ffn2_norm, ffn2_w_gate, ffn2_w_up, ffn2_w_down, final_norm):
    bsz, seq, d = x.shape
    depth = w_in.shape[0]
    t = bsz * seq
    da_w = DA_HEADS * 2 * DA_HEAD_DIM
    hg_w = HG_HEADS * HG_DIM
    xa_w = XA_HEADS * XA_HEAD_DIM
    slopes = jnp.asarray([2.0 ** (-8.0 * (i + 1) / DA_HEADS) for i in range(DA_HEADS)], F32)

    h = x.reshape(t, d)
    for l in range(depth):
        lambda_init = 0.8 - 0.6 * math.exp(-0.3 * l)
        h = _ffn(h, ffn1_norm[l], ffn1_w_gate[l].astype(BF16), ffn1_w_up[l].astype(BF16),
                 ffn1_w_down[l].astype(BF16), final_norm, final_norm=False)

        w_in16 = w_in[l].astype(BF16)
        p_att, p_hg = _in_proj(h, mix_norm[l], w_in16, da_cols=3 * da_w, hg_cols=4 * hg_w,
                               xa_cols=xa_w)
        p_att = p_att.reshape(bsz, seq, -1)
        p_hg = p_hg.reshape(bsz, seq, -1)

        lam4 = jnp.stack([da_lambda_q1[l], da_lambda_k1[l], da_lambda_q2[l], da_lambda_k2[l]])
        y_da = _diff_attn(p_att, slopes, lam4, da_subln[l], lambda_init=lambda_init)
        y_hg = _hgrn(p_hg, hg_lb_logits, hg_norm[l], layer=l)
        kv = _norm_matmul(mem.reshape(-1, d), mem_norm[l], w_mem_kv[l].astype(BF16), BF16,
                          name="mem_kv").reshape(bsz, mem.shape[1], -1)
        y_xa = _xattn(p_att, kv, q_col_block=3 * da_w // XA_HEAD_DIM)

        h = _merge(h, mix_norm[l], y_da.reshape(t, da_w), y_hg.reshape(t, hg_w),
                   y_xa.reshape(t, xa_w), w_in16, 3 * da_w + 4 * hg_w + xa_w,
                   w_branch_da[l].astype(BF16), w_branch_hg[l].astype(BF16),
                   w_branch_xa[l].astype(BF16), w_out[l].astype(BF16))

        h = _ffn(h, ffn2_norm[l], ffn2_w_gate[l].astype(BF16), ffn2_w_up[l].astype(BF16),
                 ffn2_w_down[l].astype(BF16), final_norm, final_norm=(l == depth - 1))
    return h.reshape(bsz, seq, d)
```

```python
import functools
import math

import numpy as np
import jax
import jax.numpy as jnp
from jax import lax
from jax.experimental import pallas as pl
from jax.experimental.pallas import tpu as pltpu

F32 = jnp.float32
BF16 = jnp.bfloat16

EPS = 1e-6
DA_HEADS = 4
DA_HEAD_DIM = 128
HG_HEADS = 8
HG_DIM = 128
XA_HEADS = 4
XA_HEAD_DIM = 256
N_BRANCH = 3
LOG2E = math.log2(math.e)

V7X_VMEM_BYTES = 64 * 1024 * 1024
V7X_LANES = 128

NEG_BIG = -1e30


def _vmem_limit(block_bytes):
    want = int(block_bytes * 1.25) + (4 << 20)
    return min(want, V7X_VMEM_BYTES - (4 << 20))


def _rms(x, g):
    return x * lax.rsqrt(jnp.mean(x * x, axis=-1, keepdims=True) + EPS) * g


def _dot_nt(a, b):
    return lax.dot_general(a, b, (((1,), (1,)), ((), ())), preferred_element_type=F32)


def _ffn_kernel(x_ref, g_ref, wg_ref, wu_ref, wd_ref, fg_ref, o_ref, xn_ref, *, final_norm):
    j = pl.program_id(1)

    @pl.when(j == 0)
    def _():
        xn_ref[...] = _rms(x_ref[...], g_ref[...]).astype(BF16)
        o_ref[...] = jnp.zeros_like(o_ref)

    xn = xn_ref[...]
    a = jnp.dot(xn, wg_ref[...], preferred_element_type=F32)
    b = jnp.dot(xn, wu_ref[...], preferred_element_type=F32)
    hmid = (a * jax.nn.sigmoid(a) * b).astype(BF16)
    o_ref[...] += jnp.dot(hmid, wd_ref[...], preferred_element_type=F32)

    @pl.when(j == pl.num_programs(1) - 1)
    def _():
        h = x_ref[...] + 0.5 * o_ref[...]
        if final_norm:
            h = _rms(h, fg_ref[...])
        o_ref[...] = h


def _ffn(x, gain, wg, wu, wd, final_gain, *, final_norm, tm=512, tf=512):
    t, d = x.shape
    dff = wg.shape[1]
    blocks = (2 * tm * d * 4) * 2 + tm * d * 2 + 2 * 3 * d * tf * 2 + 3 * tm * tf * 4
    return pl.pallas_call(
        functools.partial(_ffn_kernel, final_norm=final_norm),
        grid=(t // tm, dff // tf),
        in_specs=[
            pl.BlockSpec((tm, d), lambda i, j: (i, 0)),
            pl.BlockSpec((1, d), lambda i, j: (0, 0)),
            pl.BlockSpec((d, tf), lambda i, j: (0, j)),
            pl.BlockSpec((d, tf), lambda i, j: (0, j)),
            pl.BlockSpec((tf, d), lambda i, j: (j, 0)),
            pl.BlockSpec((1, d), lambda i, j: (0, 0)),
        ],
        out_specs=pl.BlockSpec((tm, d), lambda i, j: (i, 0)),
        out_shape=jax.ShapeDtypeStruct((t, d), F32),
        scratch_shapes=[pltpu.VMEM((tm, d), BF16)],
        compiler_params=pltpu.CompilerParams(
            dimension_semantics=("parallel", "arbitrary"),
            vmem_limit_bytes=_vmem_limit(blocks)),
        name="ffn_final" if final_norm else "ffn",
    )(x, gain.reshape(1, d), wg, wu, wd, final_gain.reshape(1, d))


def _norm_matmul_kernel(x_ref, g_ref, w_ref, o_ref, xn_ref):
    @pl.when(pl.program_id(1) == 0)
    def _():
        xn_ref[...] = _rms(x_ref[...], g_ref[...]).astype(BF16)

    o_ref[...] = jnp.dot(xn_ref[...], w_ref[...], preferred_element_type=F32).astype(o_ref.dtype)


def _norm_matmul(x, gain, w, out_dtype, *, tm=512, tn=1024, name):
    t, d = x.shape
    n = w.shape[1]
    out_bytes = jnp.dtype(out_dtype).itemsize
    blocks = 2 * tm * d * 4 + tm * d * 2 + 2 * d * tn * 2 + 2 * tm * tn * out_bytes
    return pl.pallas_call(
        _norm_matmul_kernel,
        grid=(t // tm, n // tn),
        in_specs=[
            pl.BlockSpec((tm, d), lambda i, j: (i, 0)),
            pl.BlockSpec((1, d), lambda i, j: (0, 0)),
            pl.BlockSpec((d, tn), lambda i, j: (0, j)),
        ],
        out_specs=pl.BlockSpec((tm, tn), lambda i, j: (i, j)),
        out_shape=jax.ShapeDtypeStruct((t, n), out_dtype),
        scratch_shapes=[pltpu.VMEM((tm, d), BF16)],
        compiler_params=pltpu.CompilerParams(
            dimension_semantics=("parallel", "arbitrary"),
            vmem_limit_bytes=_vmem_limit(blocks)),
        name=name,
    )(x, gain.reshape(1, d), w)


def _in_proj_kernel(x_ref, g_ref, w_ref, att_ref, hg_ref, xn_ref, *, n_att):
    j = pl.program_id(1)

    @pl.when(j == 0)
    def _():
        xn_ref[...] = _rms(x_ref[...], g_ref[...]).astype(BF16)

    res = jnp.dot(xn_ref[...], w_ref[...], preferred_element_type=F32)

    @pl.when(j < n_att)
    def _():
        att_ref[...] = res.astype(att_ref.dtype)

    @pl.when(j >= n_att)
    def _():
        hg_ref[...] = res


def _in_proj(x, gain, w_in, *, da_cols, hg_cols, xa_cols, tm=1024, tn=1024):
    t, d = x.shape
    da_b, hg_b, xa_b = da_cols // tn, hg_cols // tn, xa_cols // tn
    n_att = da_b + xa_b

    def w_col(j):
        return jnp.where(j < da_b, j, jnp.where(j < n_att, j + hg_b, j - xa_b))

    blocks = (2 * tm * d * 4 + tm * d * 2 + 2 * d * tn * 2 + 2 * tm * tn * 2 + 2 * tm * tn * 4
              + tm * tn * 4)
    return pl.pallas_call(
        functools.partial(_in_proj_kernel, n_att=n_att),
        grid=(t // tm, n_att + hg_b),
        in_specs=[
            pl.BlockSpec((tm, d), lambda i, j: (i, 0)),
            pl.BlockSpec((1, d), lambda i, j: (0, 0)),
            pl.BlockSpec((d, tn), lambda i, j: (0, w_col(j))),
        ],
        out_specs=[
            pl.BlockSpec((tm, tn), lambda i, j: (i, jnp.minimum(j, n_att - 1))),
            pl.BlockSpec((tm, tn), lambda i, j: (i, jnp.maximum(j - n_att, 0))),
        ],
        out_shape=[jax.ShapeDtypeStruct((t, da_cols + xa_cols), BF16),
                   jax.ShapeDtypeStruct((t, hg_cols), F32)],
        scratch_shapes=[pltpu.VMEM((tm, d), BF16)],
        compiler_params=pltpu.CompilerParams(
            dimension_semantics=("parallel", "arbitrary"),
            vmem_limit_bytes=_vmem_limit(blocks)),
        name="in_proj",
    )(x, gain.reshape(1, d), w_in)


def _diff_attn_kernel(slope_ref, lam_ref, sg_ref, q_ref, k_ref, v_ref, o_ref,
                      m_ref, l_ref, acc_ref, *, tq, tr, tk, lambda_init):
    h = pl.program_id(1)
    qi = pl.program_id(2)
    d = DA_HEAD_DIM
    hd = 2 * d
    nr = tq // tr
    scale2 = d ** -0.5 * LOG2E
    slope2 = slope_ref[h] * LOG2E

    m_ref[...] = jnp.full_like(m_ref, NEG_BIG)
    l_ref[...] = jnp.zeros_like(l_ref)
    acc_ref[...] = jnp.zeros_like(acc_ref)

    def lanes(x, width):
        return jnp.concatenate([x] * (width // V7X_LANES), axis=1)

    def tiles(row_tiles, k_start, width, masked_r):
        kb = k_ref[0, pl.ds(k_start, width), :]
        vb = v_ref[0, pl.ds(k_start, width), :]
        col = lax.broadcasted_iota(jnp.int32, (1, width), 1)
        chains = [(r, c) for r in row_tiles for c in range(2)]
        s_all = {}
        for r, c in chains:
            bias = slope2 * (col + (k_start - (qi * tq + r * tr))).astype(F32)
            q = q_ref[0, r * tr:(r + 1) * tr, c * d:(c + 1) * d]
            s = _dot_nt(q, kb[:, c * d:(c + 1) * d]) * scale2 + bias
            if r == masked_r:
                keep = (lax.broadcasted_iota(jnp.int32, (tr, width), 0)
                        >= lax.broadcasted_iota(jnp.int32, (tr, width), 1))
                s = jnp.where(keep, s, NEG_BIG)
            s_all[r, c] = s
        p_all, alpha_all = {}, {}
        for r, c in chains:
            idx = 2 * r + c
            s = s_all[r, c]
            m_old = m_ref[idx]
            m_new = jnp.maximum(m_old, jnp.max(s, axis=-1, keepdims=True))
            alpha = jnp.exp2(m_old - m_new)
            p = jnp.exp2(s - lanes(m_new, width))
            l_ref[idx] = alpha * l_ref[idx] + jnp.sum(p, axis=-1, keepdims=True)
            m_ref[idx] = m_new
            p_all[r, c] = p.astype(BF16)
            alpha_all[r, c] = alpha
        for r, c in chains:
            idx = 2 * r + c
            acc_ref[idx] = lanes(alpha_all[r, c], hd) * acc_ref[idx] + jnp.dot(
                p_all[r, c], vb, preferred_element_type=F32)

    def body(ki, carry):
        tiles(range(nr), pl.multiple_of(ki * tk, tk), tk, None)
        return carry

    lax.fori_loop(0, qi * (tq // tk), body, 0)
    for cc in range(nr):
        tiles(range(cc, nr), pl.multiple_of(qi * tq + cc * tr, tr), tr, cc)

    lam4 = lam_ref[...]
    lam = (jnp.exp(jnp.sum(lam4[0:1] * lam4[1:2], axis=-1, keepdims=True))
           - jnp.exp(jnp.sum(lam4[2:3] * lam4[3:4], axis=-1, keepdims=True)) + lambda_init)
    for r in range(nr):
        o = (acc_ref[2 * r] / lanes(l_ref[2 * r], hd)
             - lam * (acc_ref[2 * r + 1] / lanes(l_ref[2 * r + 1], hd)))
        o = _rms(o, sg_ref[...]) * (1.0 - lambda_init)
        o_ref[0, r * tr:(r + 1) * tr, :] = o.astype(o_ref.dtype)


def _diff_attn(proj, slopes, lam4, subln_g, *, lambda_init, tq=2048, tr=512, tk=512):
    b, s, _ = proj.shape
    hd = 2 * DA_HEAD_DIM
    nchain = 2 * (tq // tr)
    blocks = (2 * tq * hd * 2 + 2 * 2 * s * hd * 2 + 2 * tq * hd * 2
              + nchain * tr * (hd + 2 * V7X_LANES) * 4 + 4 * nchain * tr * tk * 4)
    return pl.pallas_call(
        functools.partial(_diff_attn_kernel, tq=tq, tr=tr, tk=tk, lambda_init=lambda_init),
        grid=(b, DA_HEADS, s // tq),
        in_specs=[
            pl.BlockSpec(memory_space=pltpu.SMEM),
            pl.BlockSpec((4, DA_HEAD_DIM), lambda bi, h, i: (0, 0)),
            pl.BlockSpec((1, hd), lambda bi, h, i: (0, 0)),
            pl.BlockSpec((1, tq, hd), lambda bi, h, i: (bi, i, h)),
            pl.BlockSpec((1, s, hd), lambda bi, h, i: (bi, 0, DA_HEADS + h)),
            pl.BlockSpec((1, s, hd), lambda bi, h, i: (bi, 0, 2 * DA_HEADS + h)),
        ],
        out_specs=pl.BlockSpec((1, tq, hd), lambda bi, h, i: (bi, i, h)),
        out_shape=jax.ShapeDtypeStruct((b, s, DA_HEADS * hd), BF16),
        scratch_shapes=[pltpu.VMEM((nchain, tr, V7X_LANES), F32),
                        pltpu.VMEM((nchain, tr, V7X_LANES), F32),
                        pltpu.VMEM((nchain, tr, hd), F32)],
        compiler_params=pltpu.CompilerParams(
            dimension_semantics=("parallel", "parallel", "arbitrary"),
            vmem_limit_bytes=_vmem_limit(blocks)),
        name="diff_attn",
    )(slopes, lam4, subln_g.reshape(1, hd), proj, proj, proj)


def _xattn_kernel(q_ref, kv_ref, o_ref):
    hd = XA_HEAD_DIM
    for h in range(XA_HEADS):
        q = q_ref[0, :, h * hd:(h + 1) * hd]
        k = kv_ref[0, :, h * hd:(h + 1) * hd]
        v = kv_ref[0, :, (XA_HEADS + h) * hd:(XA_HEADS + h + 1) * hd]
        s = _dot_nt(q, k) * (hd ** -0.5)
        p = jnp.exp(s - jnp.max(s, axis=-1, keepdims=True))
        p = p / jnp.sum(p, axis=-1, keepdims=True)
        o_ref[0, :, h * hd:(h + 1) * hd] = jnp.dot(
            p.astype(BF16), v, preferred_element_type=F32).astype(o_ref.dtype)


def _xattn(proj, kv, *, q_col_block, tq=1024):
    b, s, _ = proj.shape
    m = kv.shape[1]
    w = XA_HEADS * XA_HEAD_DIM
    blocks = 2 * 2 * tq * w * 2 + 2 * m * 2 * w * 2 + 4 * 4 * tq * m * 4
    return pl.pallas_call(
        _xattn_kernel,
        grid=(b, s // tq),
        in_specs=[
            pl.BlockSpec((1, tq, w), lambda bi, i: (bi, i, q_col_block)),
            pl.BlockSpec((1, m, 2 * w), lambda bi, i: (bi, 0, 0)),
        ],
        out_specs=pl.BlockSpec((1, tq, w), lambda bi, i: (bi, i, 0)),
        out_shape=jax.ShapeDtypeStruct((b, s, w), BF16),
        compiler_params=pltpu.CompilerParams(
            dimension_semantics=("parallel", "parallel"),
            vmem_limit_bytes=_vmem_limit(blocks)),
        name="mem_xattn",
    )(proj, kv)


HG_CHUNK = 128


def _hgrn_levels(c):
    levels, m = [], 1
    while m < c:
        levels.append(m)
        m *= 2
    return levels


def _hgrn_constants(c):
    t = np.arange(c)[:, None]
    s = np.arange(c)[None, :]
    intervals, masks = [], []
    for m in _hgrn_levels(c):
        edge = (t // (2 * m)) * (2 * m) + m - 1
        upper = (t % (2 * m)) >= m
        intervals.append(np.where(upper, (s > edge) & (s <= t), (s > t) & (s <= edge)))
        masks.append(upper & ((s // (2 * m)) == (t // (2 * m))) & ((s % (2 * m)) < m))
    intervals.append(s <= t)
    intervals.append(s > t)
    iv = np.concatenate(intervals, 0)
    return (jnp.asarray(np.concatenate([iv, iv], 1), BF16),
            jnp.asarray(np.concatenate(masks, 0), F32))


def _hgrn_kernel(lbl_ref, ng_ref, iv_ref, mk_ref, q_ref, f_ref, i_ref, g_ref, o_ref, st_ref, *,
                 ts, nh, layer):
    c = HG_CHUNK
    kd = HG_DIM
    levels = _hgrn_levels(c)
    nl = len(levels)

    @pl.when(pl.program_id(2) == 0)
    def _():
        st_ref[...] = jnp.zeros_like(st_ref)

    logits = [lbl_ref[r, 0] for r in range(lbl_ref.shape[0])]
    mx = functools.reduce(jnp.maximum, logits)
    ex = [jnp.exp(v - mx) for v in logits]
    lb = sum(ex[1:layer + 1], ex[0]) / sum(ex[1:], ex[0])
    ng = ng_ref[0]

    def head(x, hd):
        return x[:, hd * kd:(hd + 1) * kd]

    nc = ts // c
    rows = [slice(ci * c, (ci + 1) * c) for ci in range(nc)]

    qf, kk, v16, dec, a_tot = {}, {}, {}, {}, {}

    def stage1(ci):
        sl = rows[ci]
        z = f_ref[0, sl, :]
        qq = q_ref[0, sl, :]
        sig = jax.nn.sigmoid(z)
        lf2 = jnp.log(lb + (1.0 - lb) * sig) * LOG2E
        kk[ci] = (1.0 - lb) * (1.0 - sig)
        qf[ci] = qq * jax.nn.sigmoid(qq)
        v16[ci] = i_ref[0, sl, :].astype(BF16)
        hi = lf2.astype(BF16)
        lo = (lf2 - hi.astype(F32)).astype(BF16)
        parts = jnp.concatenate([hi, lo], axis=0)
        dec[ci] = [jnp.exp2(jnp.dot(iv_ref[p * c:(p + 1) * c, :], parts,
                                    preferred_element_type=F32)) for p in range(nl + 2)]

    def stage2(ci):
        acc = [jnp.zeros((c, c), F32) for _ in range(nh)]
        for li in range(nl):
            qt = (qf[ci] * dec[ci][li]).astype(BF16)
            kt = (kk[ci] * dec[ci][li]).astype(BF16)
            mask = mk_ref[li * c:(li + 1) * c, :]
            for hd in range(nh):
                acc[hd] = acc[hd] + _dot_nt(head(qt, hd), head(kt, hd)) * mask
        a_tot[ci] = [a.astype(BF16) for a in acc]

    st = [st_ref[hd] for hd in range(nh)]

    def stage3(ci):
        sl = rows[ci]
        e_b = dec[ci][nl]
        e_r = dec[ci][nl + 1]
        qb = (qf[ci] * e_b).astype(BF16)
        kr = (kk[ci] * e_r).astype(BF16)
        qk = qf[ci] * kk[ci]
        v = i_ref[0, sl, :]
        outs = []
        for hd in range(nh):
            vh = head(v16[ci], hd)
            o = jnp.sum(head(qk, hd), axis=-1, keepdims=True) * head(v, hd)
            o = o + jnp.dot(a_tot[ci][hd], vh, preferred_element_type=F32)
            o = o + _dot_nt(head(qb, hd), st[hd].astype(BF16))
            st[hd] = st[hd] * head(e_b, hd)[c - 1:c, :] + lax.dot_general(
                vh, head(kr, hd), (((0,), (0,)), ((), ())), preferred_element_type=F32)
            outs.append(o * lax.rsqrt(jnp.mean(o * o, axis=-1, keepdims=True) + EPS))
        o = jnp.concatenate(outs, axis=1) * ng * jax.nn.sigmoid(g_ref[0, sl, :])
        o_ref[0, sl, :] = o.astype(o_ref.dtype)

    for step in range(nc + 2):
        if step < nc:
            stage1(step)
        if 0 <= step - 1 < nc:
            stage2(step - 1)
        if 0 <= step - 2 < nc:
            stage3(step - 2)
    for hd in range(nh):
        st_ref[hd] = st[hd]


def _hgrn(proj, lb_logits, norm_g, *, layer, ts=1024, nh=2):
    b, s, _ = proj.shape
    kd = HG_DIM
    c = HG_CHUNK
    w = nh * kd
    ng = HG_HEADS // nh
    nl = lb_logits.shape[0]
    intervals, masks = _hgrn_constants(c)
    blocks = (2 * 4 * ts * w * 4 + 2 * ts * w * 2 + nh * kd * kd * 4 + 2 * intervals.size * 2
              + 2 * masks.size * 4 + 96 * c * w * 4)

    def col(off):
        return lambda bi, h, i: (bi, i, off * ng + h)

    return pl.pallas_call(
        functools.partial(_hgrn_kernel, ts=ts, nh=nh, layer=layer),
        grid=(b, ng, s // ts),
        in_specs=[
            pl.BlockSpec((nl, 1, 1, w), lambda bi, h, i: (0, h, 0, 0)),
            pl.BlockSpec((1, 1, w), lambda bi, h, i: (h, 0, 0)),
            pl.BlockSpec(intervals.shape, lambda bi, h, i: (0, 0)),
            pl.BlockSpec(masks.shape, lambda bi, h, i: (0, 0)),
            pl.BlockSpec((1, ts, w), col(0)),
            pl.BlockSpec((1, ts, w), col(1)),
            pl.BlockSpec((1, ts, w), col(2)),
            pl.BlockSpec((1, ts, w), col(3)),
        ],
        out_specs=pl.BlockSpec((1, ts, w), lambda bi, h, i: (bi, i, h)),
        out_shape=jax.ShapeDtypeStruct((b, s, HG_HEADS * kd), BF16),
        scratch_shapes=[pltpu.VMEM((nh, kd, kd), F32)],
        compiler_params=pltpu.CompilerParams(
            dimension_semantics=("parallel", "parallel", "arbitrary"),
            vmem_limit_bytes=_vmem_limit(blocks)),
        name="hgrn2",
    )(lb_logits.reshape(nl, ng, 1, w), norm_g.reshape(ng, 1, w), intervals, masks,
      proj, proj, proj, proj)


def _merge_kernel(h_ref, g_ref, y0_ref, y1_ref, y2_ref, wg0_ref, wg1_ref, wg2_ref,
                  wb0_ref, wb1_ref, wb2_ref, wout_ref, o_ref, u_ref, mg_ref):
    j = pl.program_id(1)
    nj = mg_ref.shape[0]
    tn = mg_ref.shape[2]

    @pl.when(j == 0)
    def _():
        u_ref[...] = _rms(h_ref[...], g_ref[...]).astype(BF16)

    u = u_ref[...]
    merged = None
    for y_ref, wg_ref, wb_ref in ((y0_ref, wg0_ref, wb0_ref), (y1_ref, wg1_ref, wb1_ref),
                                  (y2_ref, wg2_ref, wb2_ref)):
        gate = jax.nn.sigmoid(jnp.dot(u, wg_ref[...], preferred_element_type=F32))
        term = gate * jnp.dot(y_ref[...], wb_ref[...], preferred_element_type=F32)
        merged = term if merged is None else merged + term
    mg_ref[j] = merged.astype(BF16)

    @pl.when(j == nj - 1)
    def _():
        acc = h_ref[...]
        for jj in range(nj):
            acc = acc + jnp.dot(mg_ref[jj], wout_ref[jj * tn:(jj + 1) * tn, :],
                                preferred_element_type=F32)
        o_ref[...] = acc


def _merge(h, gain, y0, y1, y2, w_in, gate_col0, wb0, wb1, wb2, wout, *, tm=512, tn=256):
    t, d = h.shape
    w = y0.shape[1]
    nj = d // tn
    blocks = (2 * tm * d * 4 * 2 + tm * d * 2 + tm * d * 2 + 2 * 3 * tm * w * 2
              + 2 * 3 * d * tn * 2 + 2 * 3 * w * tn * 2 + d * d * 2 + 8 * tm * tn * 4)

    def gate_spec(br):
        off = (gate_col0 + br * d) // tn
        return pl.BlockSpec((d, tn), lambda i, j: (0, off + j))

    y_spec = pl.BlockSpec((tm, w), lambda i, j: (i, 0))
    wb_spec = pl.BlockSpec((w, tn), lambda i, j: (0, j))
    return pl.pallas_call(
        _merge_kernel,
        grid=(t // tm, nj),
        in_specs=[
            pl.BlockSpec((tm, d), lambda i, j: (i, 0)),
            pl.BlockSpec((1, d), lambda i, j: (0, 0)),
            y_spec, y_spec, y_spec,
            gate_spec(0), gate_spec(1), gate_spec(2),
            wb_spec, wb_spec, wb_spec,
            pl.BlockSpec((d, d), lambda i, j: (0, 0), pipeline_mode=pl.Buffered(1)),
        ],
        out_specs=pl.BlockSpec((tm, d), lambda i, j: (i, 0)),
        out_shape=jax.ShapeDtypeStruct((t, d), F32),
        scratch_shapes=[pltpu.VMEM((tm, d), BF16), pltpu.VMEM((nj, tm, tn), BF16)],
        compiler_params=pltpu.CompilerParams(
            dimension_semantics=("parallel", "arbitrary"),
            vmem_limit_bytes=_vmem_limit(blocks)),
        name="merge_out",
    )(h, gain.reshape(1, d), y0, y1, y2, w_in, w_in, w_in, wb0, wb1, wb2, wout)


def kernel(x, mem, ffn1_norm, ffn1_w_gate, ffn1_w_up, ffn1_w_down, mix_norm, mem_norm, w_in,
           da_lambda_q1, da_lambda_k1, da_lambda_q2, da_lambda_k2, da_subln, hg_lb_logits, hg_norm,
           w_mem_kv, w_branch_da, w_branch_hg, w_branch_xa, w_out,
           ffn2_norm, ffn2_w_gate, ffn2_w_up, ffn2_w_down, final_norm):
    bsz, seq, d = x.shape
    depth = w_in.shape[0]
    t = bsz * seq
    da_w = DA_HEADS * 2 * DA_HEAD_DIM
    hg_w = HG_HEADS * HG_DIM
    xa_w = XA_HEADS * XA_HEAD_DIM
    slopes = jnp.asarray([2.0 ** (-8.0 * (i + 1) / DA_HEADS) for i in range(DA_HEADS)], F32)

    h = x.reshape(t, d)
    for l in range(depth):
        lambda_init = 0.8 - 0.6 * math.exp(-0.3 * l)
        h = _ffn(h, ffn1_norm[l], ffn1_w_gate[l].astype(BF16), ffn1_w_up[l].astype(BF16),
                 ffn1_w_down[l].astype(BF16), final_norm, final_norm=False)

        w_in16 = w_in[l].astype(BF16)
        p_att, p_hg = _in_proj(h, mix_norm[l], w_in16, da_cols=3 * da_w, hg_cols=4 * hg_w,
                               xa_cols=xa_w)
        p_att = p_att.reshape(bsz, seq, -1)
        p_hg = p_hg.reshape(bsz, seq, -1)

        lam4 = jnp.stack([da_lambda_q1[l], da_lambda_k1[l], da_lambda_q2[l], da_lambda_k2[l]])
        y_da = _diff_attn(p_att, slopes, lam4, da_subln[l], lambda_init=lambda_init)
        y_hg = _hgrn(p_hg, hg_lb_logits, hg_norm[l], layer=l)
        kv = _norm_matmul(mem.reshape(-1, d), mem_norm[l], w_mem_kv[l].astype(BF16), BF16,
                          name="mem_kv").reshape(bsz, mem.shape[1], -1)
        y_xa = _xattn(p_att, kv, q_col_block=3 * da_w // xa_w)

        h = _merge(h, mix_norm[l], y_da.reshape(t, da_w), y_hg.reshape(t, hg_w),
                   y_xa.reshape(t, xa_w), w_in16, 3 * da_w + 4 * hg_w + xa_w,
                   w_branch_da[l].astype(BF16), w_branch_hg[l].astype(BF16),
                   w_branch_xa[l].astype(BF16), w_out[l].astype(BF16))

        h = _ffn(h, ffn2_norm[l], ffn2_w_gate[l].astype(BF16), ffn2_w_up[l].astype(BF16),
                 ffn2_w_down[l].astype(BF16), final_norm, final_norm=(l == depth - 1))
    return h.reshape(bsz, seq, d)
```

```python
import functools
import math

import numpy as np
import jax
import jax.numpy as jnp
from jax import lax
from jax.experimental import pallas as pl
from jax.experimental.pallas import tpu as pltpu

F32 = jnp.float32
BF16 = jnp.bfloat16

EPS = 1e-6
DA_HEADS = 4
DA_HEAD_DIM = 128
HG_HEADS = 8
HG_DIM = 128
XA_HEADS = 4
XA_HEAD_DIM = 256
N_BRANCH = 3
LOG2E = math.log2(math.e)

V7X_VMEM_BYTES = 64 * 1024 * 1024
V7X_LANES = 128

NEG_BIG = -1e30


def _vmem_limit(block_bytes):
    want = int(block_bytes * 1.25) + (4 << 20)
    return min(want, V7X_VMEM_BYTES - (4 << 20))


def _rms(x, g):
    return x * lax.rsqrt(jnp.mean(x * x, axis=-1, keepdims=True) + EPS) * g


def _dot_nt(a, b):
    return lax.dot_general(a, b, (((1,), (1,)), ((), ())), preferred_element_type=F32)


def _ffn_kernel(x_hbm, g_ref, wg_ref, wu_ref, wd_ref, fg_ref, o_ref, x_buf, xn_ref, x_sem, *,
                final_norm):
    i = pl.program_id(0)
    j = pl.program_id(1)
    tm = x_buf.shape[0]

    def x_copy(tile):
        return pltpu.make_async_copy(x_hbm.at[pl.ds(tile * tm, tm), :], x_buf, x_sem)

    @pl.when((j == 0) & (i == 0))
    def _():
        x_copy(0).start()

    @pl.when(j == 0)
    def _():
        x_copy(i).wait()
        xn_ref[...] = _rms(x_buf[...], g_ref[...]).astype(BF16)
        o_ref[...] = 2.0 * x_buf[...]

    @pl.when((j == 1) & (i + 1 < pl.num_programs(0)))
    def _():
        x_copy(i + 1).start()

    xn = xn_ref[...]
    a = jnp.dot(xn, wg_ref[...], preferred_element_type=F32)
    b = jnp.dot(xn, wu_ref[...], preferred_element_type=F32)
    hmid = (a * jax.nn.sigmoid(a) * b).astype(BF16)
    o_ref[...] += jnp.dot(hmid, wd_ref[...], preferred_element_type=F32)

    @pl.when(j == pl.num_programs(1) - 1)
    def _():
        h = 0.5 * o_ref[...]
        if final_norm:
            h = _rms(h, fg_ref[...])
        o_ref[...] = h


def _ffn(x, gain, wg, wu, wd, final_gain, *, final_norm, tm=1024, tf=512):
    t, d = x.shape
    dff = wg.shape[1]
    assert dff // tf >= 2
    blocks = 3 * tm * d * 4 + tm * d * 2 + 2 * 3 * d * tf * 2 + 3 * tm * tf * 4
    return pl.pallas_call(
        functools.partial(_ffn_kernel, final_norm=final_norm),
        grid=(t // tm, dff // tf),
        in_specs=[
            pl.BlockSpec(memory_space=pl.ANY),
            pl.BlockSpec((1, d), lambda i, j: (0, 0)),
            pl.BlockSpec((d, tf), lambda i, j: (0, j)),
            pl.BlockSpec((d, tf), lambda i, j: (0, j)),
            pl.BlockSpec((tf, d), lambda i, j: (j, 0)),
            pl.BlockSpec((1, d), lambda i, j: (0, 0)),
        ],
        out_specs=pl.BlockSpec((tm, d), lambda i, j: (i, 0)),
        out_shape=jax.ShapeDtypeStruct((t, d), F32),
        scratch_shapes=[pltpu.VMEM((tm, d), F32), pltpu.VMEM((tm, d), BF16),
                        pltpu.SemaphoreType.DMA(())],
        compiler_params=pltpu.CompilerParams(
            dimension_semantics=("arbitrary", "arbitrary"),
            vmem_limit_bytes=_vmem_limit(blocks)),
        name="ffn_final" if final_norm else "ffn",
    )(x, gain.reshape(1, d), wg, wu, wd, final_gain.reshape(1, d))


def _norm_matmul_kernel(x_ref, g_ref, w_ref, o_ref, xn_ref):
    @pl.when(pl.program_id(1) == 0)
    def _():
        xn_ref[...] = _rms(x_ref[...], g_ref[...]).astype(BF16)

    o_ref[...] = jnp.dot(xn_ref[...], w_ref[...], preferred_element_type=F32).astype(o_ref.dtype)


def _norm_matmul(x, gain, w, out_dtype, *, tm=512, tn=1024, name):
    t, d = x.shape
    n = w.shape[1]
    out_bytes = jnp.dtype(out_dtype).itemsize
    blocks = 2 * tm * d * 4 + tm * d * 2 + 2 * d * tn * 2 + 2 * tm * tn * out_bytes
    return pl.pallas_call(
        _norm_matmul_kernel,
        grid=(t // tm, n // tn),
        in_specs=[
            pl.BlockSpec((tm, d), lambda i, j: (i, 0)),
            pl.BlockSpec((1, d), lambda i, j: (0, 0)),
            pl.BlockSpec((d, tn), lambda i, j: (0, j)),
        ],
        out_specs=pl.BlockSpec((tm, tn), lambda i, j: (i, j)),
        out_shape=jax.ShapeDtypeStruct((t, n), out_dtype),
        scratch_shapes=[pltpu.VMEM((tm, d), BF16)],
        compiler_params=pltpu.CompilerParams(
            dimension_semantics=("parallel", "arbitrary"),
            vmem_limit_bytes=_vmem_limit(blocks)),
        name=name,
    )(x, gain.reshape(1, d), w)


def _in_proj_kernel(x_ref, g_ref, w_ref, att_ref, hg_ref, xn_ref, *, n_att):
    j = pl.program_id(1)

    @pl.when(j == 0)
    def _():
        xn_ref[...] = _rms(x_ref[...], g_ref[...]).astype(BF16)

    res = jnp.dot(xn_ref[...], w_ref[...], preferred_element_type=F32)

    @pl.when(j < n_att)
    def _():
        att_ref[...] = res.astype(att_ref.dtype)

    @pl.when(j >= n_att)
    def _():
        hg_ref[...] = res


def _in_proj(x, gain, w_in, *, da_cols, hg_cols, xa_cols, tm=1024, tn=1024):
    t, d = x.shape
    da_b, hg_b, xa_b = da_cols // tn, hg_cols // tn, xa_cols // tn
    n_att = da_b + xa_b

    def w_col(j):
        return jnp.where(j < da_b, j, jnp.where(j < n_att, j + hg_b, j - xa_b))

    blocks = (2 * tm * d * 4 + tm * d * 2 + 2 * d * tn * 2 + 2 * tm * tn * 2 + 2 * tm * tn * 4
              + tm * tn * 4)
    return pl.pallas_call(
        functools.partial(_in_proj_kernel, n_att=n_att),
        grid=(t // tm, n_att + hg_b),
        in_specs=[
            pl.BlockSpec((tm, d), lambda i, j: (i, 0)),
            pl.BlockSpec((1, d), lambda i, j: (0, 0)),
            pl.BlockSpec((d, tn), lambda i, j: (0, w_col(j))),
        ],
        out_specs=[
            pl.BlockSpec((tm, tn), lambda i, j: (i, jnp.minimum(j, n_att - 1))),
            pl.BlockSpec((tm, tn), lambda i, j: (i, jnp.maximum(j - n_att, 0))),
        ],
        out_shape=[jax.ShapeDtypeStruct((t, da_cols + xa_cols), BF16),
                   jax.ShapeDtypeStruct((t, hg_cols), F32)],
        scratch_shapes=[pltpu.VMEM((tm, d), BF16)],
        compiler_params=pltpu.CompilerParams(
            dimension_semantics=("parallel", "arbitrary"),
            vmem_limit_bytes=_vmem_limit(blocks)),
        name="in_proj",
    )(x, gain.reshape(1, d), w_in)


def _diff_attn_kernel(slope_ref, lam_ref, sg_ref, q_ref, k_ref, v_ref, o_ref,
                      m_ref, l_ref, acc_ref, *, tq, tr, tk, lambda_init):
    h = pl.program_id(1)
    qi = pl.program_id(2)
    d = DA_HEAD_DIM
    hd = 2 * d
    nr = tq // tr
    scale2 = d ** -0.5 * LOG2E
    slope2 = slope_ref[h] * LOG2E

    m_ref[...] = jnp.full_like(m_ref, NEG_BIG)
    l_ref[...] = jnp.zeros_like(l_ref)
    acc_ref[...] = jnp.zeros_like(acc_ref)

    def lanes(x, width):
        return jnp.concatenate([x] * (width // V7X_LANES), axis=1)

    def tiles(row_tiles, k_start, width, masked_r):
        kb = k_ref[0, pl.ds(k_start, width), :]
        vb = v_ref[0, pl.ds(k_start, width), :]
        col = lax.broadcasted_iota(jnp.int32, (1, width), 1)
        chains = [(r, c) for r in row_tiles for c in range(2)]
        s_all = {}
        for r, c in chains:
            bias = slope2 * (col + (k_start - (qi * tq + r * tr))).astype(F32)
            q = q_ref[0, r * tr:(r + 1) * tr, c * d:(c + 1) * d]
            s = _dot_nt(q, kb[:, c * d:(c + 1) * d]) * scale2 + bias
            if r == masked_r:
                keep = (lax.broadcasted_iota(jnp.int32, (tr, width), 0)
                        >= lax.broadcasted_iota(jnp.int32, (tr, width), 1))
                s = jnp.where(keep, s, NEG_BIG)
            s_all[r, c] = s
        p_all, alpha_all = {}, {}
        for r, c in chains:
            idx = 2 * r + c
            s = s_all[r, c]
            m_old = m_ref[idx]
            m_new = jnp.maximum(m_old, jnp.max(s, axis=-1, keepdims=True))
            alpha = jnp.exp2(m_old - m_new)
            p = jnp.exp2(s - lanes(m_new, width))
            l_ref[idx] = alpha * l_ref[idx] + jnp.sum(p, axis=-1, keepdims=True)
            m_ref[idx] = m_new
            p_all[r, c] = p.astype(BF16)
            alpha_all[r, c] = alpha
        for r, c in chains:
            idx = 2 * r + c
            acc_ref[idx] = lanes(alpha_all[r, c], hd) * acc_ref[idx] + jnp.dot(
                p_all[r, c], vb, preferred_element_type=F32)

    def body(ki, carry):
        tiles(range(nr), pl.multiple_of(ki * tk, tk), tk, None)
        return carry

    lax.fori_loop(0, qi * (tq // tk), body, 0)
    for cc in range(nr):
        tiles(range(cc, nr), pl.multiple_of(qi * tq + cc * tr, tr), tr, cc)

    lam4 = lam_ref[...]
    lam = (jnp.exp(jnp.sum(lam4[0:1] * lam4[1:2], axis=-1, keepdims=True))
           - jnp.exp(jnp.sum(lam4[2:3] * lam4[3:4], axis=-1, keepdims=True)) + lambda_init)
    for r in range(nr):
        o = (acc_ref[2 * r] / lanes(l_ref[2 * r], hd)
             - lam * (acc_ref[2 * r + 1] / lanes(l_ref[2 * r + 1], hd)))
        o = _rms(o, sg_ref[...]) * (1.0 - lambda_init)
        o_ref[0, r * tr:(r + 1) * tr, :] = o.astype(o_ref.dtype)


def _diff_attn(proj, slopes, lam4, subln_g, *, lambda_init, tq=2048, tr=512, tk=512):
    b, s, _ = proj.shape
    hd = 2 * DA_HEAD_DIM
    nchain = 2 * (tq // tr)
    blocks = (2 * tq * hd * 2 + 2 * 2 * s * hd * 2 + 2 * tq * hd * 2
              + nchain * tr * (hd + 2 * V7X_LANES) * 4 + 4 * nchain * tr * tk * 4)
    return pl.pallas_call(
        functools.partial(_diff_attn_kernel, tq=tq, tr=tr, tk=tk, lambda_init=lambda_init),
        grid=(b, DA_HEADS, s // tq),
        in_specs=[
            pl.BlockSpec(memory_space=pltpu.SMEM),
            pl.BlockSpec((4, DA_HEAD_DIM), lambda bi, h, i: (0, 0)),
            pl.BlockSpec((1, hd), lambda bi, h, i: (0, 0)),
            pl.BlockSpec((1, tq, hd), lambda bi, h, i: (bi, i, h)),
            pl.BlockSpec((1, s, hd), lambda bi, h, i: (bi, 0, DA_HEADS + h)),
            pl.BlockSpec((1, s, hd), lambda bi, h, i: (bi, 0, 2 * DA_HEADS + h)),
        ],
        out_specs=pl.BlockSpec((1, tq, hd), lambda bi, h, i: (bi, i, h)),
        out_shape=jax.ShapeDtypeStruct((b, s, DA_HEADS * hd), BF16),
        scratch_shapes=[pltpu.VMEM((nchain, tr, V7X_LANES), F32),
                        pltpu.VMEM((nchain, tr, V7X_LANES), F32),
                        pltpu.VMEM((nchain, tr, hd), F32)],
        compiler_params=pltpu.CompilerParams(
            dimension_semantics=("parallel", "parallel", "arbitrary"),
            vmem_limit_bytes=_vmem_limit(blocks)),
        name="diff_attn",
    )(slopes, lam4, subln_g.reshape(1, hd), proj, proj, proj)


def _xattn_kernel(q_ref, kv_ref, o_ref):
    hd = XA_HEAD_DIM
    for h in range(XA_HEADS):
        q = q_ref[0, :, h * hd:(h + 1) * hd]
        k = kv_ref[0, :, h * hd:(h + 1) * hd]
        v = kv_ref[0, :, (XA_HEADS + h) * hd:(XA_HEADS + h + 1) * hd]
        s = _dot_nt(q, k) * (hd ** -0.5)
        p = jnp.exp(s - jnp.max(s, axis=-1, keepdims=True))
        p = p / jnp.sum(p, axis=-1, keepdims=True)
        o_ref[0, :, h * hd:(h + 1) * hd] = jnp.dot(
            p.astype(BF16), v, preferred_element_type=F32).astype(o_ref.dtype)


def _xattn(proj, kv, *, q_col_block, tq=1024):
    b, s, _ = proj.shape
    m = kv.shape[1]
    w = XA_HEADS * XA_HEAD_DIM
    blocks = 2 * 2 * tq * w * 2 + 2 * m * 2 * w * 2 + 4 * 4 * tq * m * 4
    return pl.pallas_call(
        _xattn_kernel,
        grid=(b, s // tq),
        in_specs=[
            pl.BlockSpec((1, tq, w), lambda bi, i: (bi, i, q_col_block)),
            pl.BlockSpec((1, m, 2 * w), lambda bi, i: (bi, 0, 0)),
        ],
        out_specs=pl.BlockSpec((1, tq, w), lambda bi, i: (bi, i, 0)),
        out_shape=jax.ShapeDtypeStruct((b, s, w), BF16),
        compiler_params=pltpu.CompilerParams(
            dimension_semantics=("parallel", "parallel"),
            vmem_limit_bytes=_vmem_limit(blocks)),
        name="mem_xattn",
    )(proj, kv)


HG_CHUNK = 128


def _hgrn_levels(c):
    levels, m = [], 1
    while m < c:
        levels.append(m)
        m *= 2
    return levels


def _hgrn_constants(c):
    t = np.arange(c)[:, None]
    s = np.arange(c)[None, :]
    intervals, masks = [], []
    for m in _hgrn_levels(c):
        edge = (t // (2 * m)) * (2 * m) + m - 1
        upper = (t % (2 * m)) >= m
        intervals.append(np.where(upper, (s > edge) & (s <= t), (s > t) & (s <= edge)))
        masks.append(upper & ((s // (2 * m)) == (t // (2 * m))) & ((s % (2 * m)) < m))
    intervals.append(s <= t)
    intervals.append(s > t)
    iv = np.concatenate(intervals, 0)
    return (jnp.asarray(np.concatenate([iv, iv], 1), BF16),
            jnp.asarray(np.concatenate(masks, 0), F32))


def _hgrn_kernel(lbl_ref, ng_ref, iv_ref, mk_ref, q_ref, f_ref, i_ref, g_ref, o_ref, st_ref, *,
                 ts, nh, layer):
    c = HG_CHUNK
    kd = HG_DIM
    levels = _hgrn_levels(c)
    nl = len(levels)

    @pl.when(pl.program_id(2) == 0)
    def _():
        st_ref[...] = jnp.zeros_like(st_ref)

    logits = [lbl_ref[r, 0] for r in range(lbl_ref.shape[0])]
    mx = functools.reduce(jnp.maximum, logits)
    ex = [jnp.exp(v - mx) for v in logits]
    lb = sum(ex[1:layer + 1], ex[0]) / sum(ex[1:], ex[0])
    ng = ng_ref[0]

    def head(x, hd):
        return x[:, hd * kd:(hd + 1) * kd]

    nc = ts // c
    rows = [slice(ci * c, (ci + 1) * c) for ci in range(nc)]

    qf, kk, v16, dec, a_tot = {}, {}, {}, {}, {}

    def stage1(ci):
        sl = rows[ci]
        z = f_ref[0, sl, :]
        qq = q_ref[0, sl, :]
        sig = jax.nn.sigmoid(z)
        lf2 = jnp.log(lb + (1.0 - lb) * sig) * LOG2E
        kk[ci] = (1.0 - lb) * (1.0 - sig)
        qf[ci] = qq * jax.nn.sigmoid(qq)
        v16[ci] = i_ref[0, sl, :].astype(BF16)
        hi = lf2.astype(BF16)
        lo = (lf2 - hi.astype(F32)).astype(BF16)
        parts = jnp.concatenate([hi, lo], axis=0)
        dec[ci] = [jnp.exp2(jnp.dot(iv_ref[p * c:(p + 1) * c, :], parts,
                                    preferred_element_type=F32)) for p in range(nl + 2)]

    def stage2(ci):
        acc = [jnp.zeros((c, c), F32) for _ in range(nh)]
        for li in range(nl):
            qt = (qf[ci] * dec[ci][li]).astype(BF16)
            kt = (kk[ci] * dec[ci][li]).astype(BF16)
            mask = mk_ref[li * c:(li + 1) * c, :]
            for hd in range(nh):
                acc[hd] = acc[hd] + _dot_nt(head(qt, hd), head(kt, hd)) * mask
        a_tot[ci] = [a.astype(BF16) for a in acc]

    st = [st_ref[hd] for hd in range(nh)]

    def stage3(ci):
        sl = rows[ci]
        e_b = dec[ci][nl]
        e_r = dec[ci][nl + 1]
        qb = (qf[ci] * e_b).astype(BF16)
        kr = (kk[ci] * e_r).astype(BF16)
        qk = qf[ci] * kk[ci]
        v = i_ref[0, sl, :]
        outs = []
        for hd in range(nh):
            vh = head(v16[ci], hd)
            o = jnp.sum(head(qk, hd), axis=-1, keepdims=True) * head(v, hd)
            o = o + jnp.dot(a_tot[ci][hd], vh, preferred_element_type=F32)
            o = o + _dot_nt(head(qb, hd), st[hd].astype(BF16))
            st[hd] = st[hd] * head(e_b, hd)[c - 1:c, :] + lax.dot_general(
                vh, head(kr, hd), (((0,), (0,)), ((), ())), preferred_element_type=F32)
            outs.append(o * lax.rsqrt(jnp.mean(o * o, axis=-1, keepdims=True) + EPS))
        o = jnp.concatenate(outs, axis=1) * ng * jax.nn.sigmoid(g_ref[0, sl, :])
        o_ref[0, sl, :] = o.astype(o_ref.dtype)

    for step in range(nc + 2):
        if step < nc:
            stage1(step)
        if 0 <= step - 1 < nc:
            stage2(step - 1)
        if 0 <= step - 2 < nc:
            stage3(step - 2)
    for hd in range(nh):
        st_ref[hd] = st[hd]


def _hgrn(proj, lb_logits, norm_g, *, layer, ts=1024, nh=2):
    b, s, _ = proj.shape
    kd = HG_DIM
    c = HG_CHUNK
    w = nh * kd
    ng = HG_HEADS // nh
    nl = lb_logits.shape[0]
    intervals, masks = _hgrn_constants(c)
    blocks = (2 * 4 * ts * w * 4 + 2 * ts * w * 2 + nh * kd * kd * 4 + 2 * intervals.size * 2
              + 2 * masks.size * 4 + 96 * c * w * 4)

    def col(off):
        return lambda bi, h, i: (bi, i, off * ng + h)

    return pl.pallas_call(
        functools.partial(_hgrn_kernel, ts=ts, nh=nh, layer=layer),
        grid=(b, ng, s // ts),
        in_specs=[
            pl.BlockSpec((nl, 1, 1, w), lambda bi, h, i: (0, h, 0, 0)),
            pl.BlockSpec((1, 1, w), lambda bi, h, i: (h, 0, 0)),
            pl.BlockSpec(intervals.shape, lambda bi, h, i: (0, 0)),
            pl.BlockSpec(masks.shape, lambda bi, h, i: (0, 0)),
            pl.BlockSpec((1, ts, w), col(0)),
            pl.BlockSpec((1, ts, w), col(1)),
            pl.BlockSpec((1, ts, w), col(2)),
            pl.BlockSpec((1, ts, w), col(3)),
        ],
        out_specs=pl.BlockSpec((1, ts, w), lambda bi, h, i: (bi, i, h)),
        out_shape=jax.ShapeDtypeStruct((b, s, HG_HEADS * kd), BF16),
        scratch_shapes=[pltpu.VMEM((nh, kd, kd), F32)],
        compiler_params=pltpu.CompilerParams(
            dimension_semantics=("parallel", "parallel", "arbitrary"),
            vmem_limit_bytes=_vmem_limit(blocks)),
        name="hgrn2",
    )(lb_logits.reshape(nl, ng, 1, w), norm_g.reshape(ng, 1, w), intervals, masks,
      proj, proj, proj, proj)


def _merge_kernel(h_ref, g_ref, y0_ref, y1_ref, y2_ref, wg0_ref, wg1_ref, wg2_ref,
                  wb0_ref, wb1_ref, wb2_ref, wout_ref, o_ref, u_ref, mg_ref):
    j = pl.program_id(1)
    nj = mg_ref.shape[0]
    tn = mg_ref.shape[2]

    @pl.when(j == 0)
    def _():
        u_ref[...] = _rms(h_ref[...], g_ref[...]).astype(BF16)

    u = u_ref[...]
    merged = None
    for y_ref, wg_ref, wb_ref in ((y0_ref, wg0_ref, wb0_ref), (y1_ref, wg1_ref, wb1_ref),
                                  (y2_ref, wg2_ref, wb2_ref)):
        gate = jax.nn.sigmoid(jnp.dot(u, wg_ref[...], preferred_element_type=F32))
        term = gate * jnp.dot(y_ref[...], wb_ref[...], preferred_element_type=F32)
        merged = term if merged is None else merged + term
    mg_ref[j] = merged.astype(BF16)

    @pl.when(j == nj - 1)
    def _():
        acc = h_ref[...]
        for jj in range(nj):
            acc = acc + jnp.dot(mg_ref[jj], wout_ref[jj * tn:(jj + 1) * tn, :],
                                preferred_element_type=F32)
        o_ref[...] = acc


def _merge(h, gain, y0, y1, y2, w_in, gate_col0, wb0, wb1, wb2, wout, *, tm=512, tn=256):
    t, d = h.shape
    w = y0.shape[1]
    nj = d // tn
    blocks = (2 * tm * d * 4 * 2 + tm * d * 2 + tm * d * 2 + 2 * 3 * tm * w * 2
              + 2 * 3 * d * tn * 2 + 2 * 3 * w * tn * 2 + d * d * 2 + 8 * tm * tn * 4)

    def gate_spec(br):
        off = (gate_col0 + br * d) // tn
        return pl.BlockSpec((d, tn), lambda i, j: (0, off + j))

    y_spec = pl.BlockSpec((tm, w), lambda i, j: (i, 0))
    wb_spec = pl.BlockSpec((w, tn), lambda i, j: (0, j))
    return pl.pallas_call(
        _merge_kernel,
        grid=(t // tm, nj),
        in_specs=[
            pl.BlockSpec((tm, d), lambda i, j: (i, 0)),
            pl.BlockSpec((1, d), lambda i, j: (0, 0)),
            y_spec, y_spec, y_spec,
            gate_spec(0), gate_spec(1), gate_spec(2),
            wb_spec, wb_spec, wb_spec,
            pl.BlockSpec((d, d), lambda i, j: (0, 0), pipeline_mode=pl.Buffered(1)),
        ],
        out_specs=pl.BlockSpec((tm, d), lambda i, j: (i, 0)),
        out_shape=jax.ShapeDtypeStruct((t, d), F32),
        scratch_shapes=[pltpu.VMEM((tm, d), BF16), pltpu.VMEM((nj, tm, tn), BF16)],
        compiler_params=pltpu.CompilerParams(
            dimension_semantics=("parallel", "arbitrary"),
            vmem_limit_bytes=_vmem_limit(blocks)),
        name="merge_out",
    )(h, gain.reshape(1, d), y0, y1, y2, w_in, w_in, w_in, wb0, wb1, wb2, wout)


def kernel(x, mem, ffn1_norm, ffn1_w_gate, ffn1_w_up, ffn1_w_down, mix_norm, mem_norm, w_in,
           da_lambda_q1, da_lambda_k1, da_lambda_q2, da_lambda_k2, da_subln, hg_lb_logits, hg_norm,
           w_mem_kv, w_branch_da, w_branch_hg, w_branch_xa, w_out,
           ffn2_norm, ffn2_w_gate, ffn2_w_up, ffn2_w_down, final_norm):
    bsz, seq, d = x.shape
    depth = w_in.shape[0]
    t = bsz * seq
    da_w = DA_HEADS * 2 * DA_HEAD_DIM
    hg_w = HG_HEADS * HG_DIM
    xa_w = XA_HEADS * XA_HEAD_DIM
    slopes = jnp.asarray([2.0 ** (-8.0 * (i + 1) / DA_HEADS) for i in range(DA_HEADS)], F32)

    h = x.reshape(t, d)
    for l in range(depth):
        lambda_init = 0.8 - 0.6 * math.exp(-0.3 * l)
        h = _ffn(h, ffn1_norm[l], ffn1_w_gate[l].astype(BF16), ffn1_w_up[l].astype(BF16),
                 ffn1_w_down[l].astype(BF16), final_norm, final_norm=False)

        w_in16 = w_in[l].astype(BF16)
        p_att, p_hg = _in_proj(h, mix_norm[l], w_in16, da_cols=3 * da_w, hg_cols=4 * hg_w,
                               xa_cols=xa_w)
        p_att = p_att.reshape(bsz, seq, -1)
        p_hg = p_hg.reshape(bsz, seq, -1)

        lam4 = jnp.stack([da_lambda_q1[l], da_lambda_k1[l], da_lambda_q2[l], da_lambda_k2[l]])
        y_da = _diff_attn(p_att, slopes, lam4, da_subln[l], lambda_init=lambda_init)
        y_hg = _hgrn(p_hg, hg_lb_logits, hg_norm[l], layer=l)
        kv = _norm_matmul(mem.reshape(-1, d), mem_norm[l], w_mem_kv[l].astype(BF16), BF16,
                          name="mem_kv").reshape(bsz, mem.shape[1], -1)
        y_xa = _xattn(p_att, kv, q_col_block=3 * da_w // xa_w)

        h = _merge(h, mix_norm[l], y_da.reshape(t, da_w), y_hg.reshape(t, hg_w),
                   y_xa.reshape(t, xa_w), w_in16, 3 * da_w + 4 * hg_w + xa_w,
                   w_branch_da[l].astype(BF16), w_branch_hg[l].astype(BF16),
                   w_branch_xa[l].astype(BF16), w_out[l].astype(BF16))

        h = _ffn(h, ffn2_norm[l], ffn2_w_gate[l].astype(BF16), ffn2_w_up[l].astype(BF16),
                 ffn2_w_down[l].astype(BF16), final_norm, final_norm=(l == depth - 1))
    return h.reshape(bsz, seq, d)
```

```python
import functools
import math

import numpy as np
import jax
import jax.numpy as jnp
from jax import lax
from jax.experimental import pallas as pl
from jax.experimental.pallas import tpu as pltpu

F32 = jnp.float32
BF16 = jnp.bfloat16

EPS = 1e-6
DA_HEADS = 4
DA_HEAD_DIM = 128
HG_HEADS = 8
HG_DIM = 128
XA_HEADS = 4
XA_HEAD_DIM = 256
N_BRANCH = 3
LOG2E = math.log2(math.e)
PROJ_TN = 1024

V7X_VMEM_BYTES = 64 * 1024 * 1024
V7X_LANES = 128

NEG_BIG = -1e30


def _vmem_limit(block_bytes):
    want = int(block_bytes * 1.25) + (4 << 20)
    return min(want, V7X_VMEM_BYTES - (4 << 20))


def _rms(x, g):
    return x * lax.rsqrt(jnp.mean(x * x, axis=-1, keepdims=True) + EPS) * g


def _dot_nt(a, b):
    return lax.dot_general(a, b, (((1,), (1,)), ((), ())), preferred_element_type=F32)


def _ffn_kernel(x_hbm, g_ref, wg_ref, wu_ref, wd_ref, fg_ref, o_ref, x_buf, xn_ref, x_sem, *,
                final_norm):
    i = pl.program_id(0)
    j = pl.program_id(1)
    tm = x_buf.shape[0]

    def x_copy(tile):
        return pltpu.make_async_copy(x_hbm.at[pl.ds(tile * tm, tm), :], x_buf, x_sem)

    @pl.when((j == 0) & (i == 0))
    def _():
        x_copy(0).start()

    @pl.when(j == 0)
    def _():
        x_copy(i).wait()
        xn_ref[...] = _rms(x_buf[...], g_ref[...]).astype(BF16)
        o_ref[...] = x_buf[...]

    @pl.when((j == 1) & (i + 1 < pl.num_programs(0)))
    def _():
        x_copy(i + 1).start()

    xn = xn_ref[...]
    a = jnp.dot(xn, wg_ref[...], preferred_element_type=F32)
    b = jnp.dot(xn, wu_ref[...], preferred_element_type=F32)
    hmid = (a * jax.nn.sigmoid(a) * b).astype(BF16)
    o_ref[...] += jnp.dot(hmid, wd_ref[...], preferred_element_type=F32)

    if final_norm:
        @pl.when(j == pl.num_programs(1) - 1)
        def _():
            o_ref[...] = _rms(o_ref[...], fg_ref[...])


def _ffn(x, gain, wg, wu, wd_half, final_gain, *, final_norm, tm=1024, tf=512):
    wd = wd_half
    t, d = x.shape
    dff = wg.shape[1]
    assert dff // tf >= 2
    blocks = 3 * tm * d * 4 + tm * d * 2 + 2 * 3 * d * tf * 2 + 3 * tm * tf * 4
    return pl.pallas_call(
        functools.partial(_ffn_kernel, final_norm=final_norm),
        grid=(t // tm, dff // tf),
        in_specs=[
            pl.BlockSpec(memory_space=pl.ANY),
            pl.BlockSpec((1, d), lambda i, j: (0, 0)),
            pl.BlockSpec((d, tf), lambda i, j: (0, j)),
            pl.BlockSpec((d, tf), lambda i, j: (0, j)),
            pl.BlockSpec((tf, d), lambda i, j: (j, 0)),
            pl.BlockSpec((1, d), lambda i, j: (0, 0)),
        ],
        out_specs=pl.BlockSpec((tm, d), lambda i, j: (i, 0)),
        out_shape=jax.ShapeDtypeStruct((t, d), F32),
        scratch_shapes=[pltpu.VMEM((tm, d), F32), pltpu.VMEM((tm, d), BF16),
                        pltpu.SemaphoreType.DMA(())],
        compiler_params=pltpu.CompilerParams(
            dimension_semantics=("arbitrary", "arbitrary"),
            vmem_limit_bytes=_vmem_limit(blocks)),
        name="ffn_final" if final_norm else "ffn",
    )(x, gain.reshape(1, d), wg, wu, wd, final_gain.reshape(1, d))


def _norm_matmul_kernel(x_ref, g_ref, w_ref, o_ref, xn_ref):
    @pl.when(pl.program_id(1) == 0)
    def _():
        xn_ref[...] = _rms(x_ref[...], g_ref[...]).astype(BF16)

    o_ref[...] = jnp.dot(xn_ref[...], w_ref[...], preferred_element_type=F32).astype(o_ref.dtype)


def _norm_matmul(x, gain, w, out_dtype, *, n_cols, w_col=lambda j: j, tm, tn=1024, name):
    t, d = x.shape
    out_bytes = jnp.dtype(out_dtype).itemsize
    blocks = (2 * tm * d * 4 + tm * d * 2 + 2 * d * tn * 2 + 2 * tm * tn * out_bytes
              + tm * tn * 4)
    return pl.pallas_call(
        _norm_matmul_kernel,
        grid=(t // tm, n_cols // tn),
        in_specs=[
            pl.BlockSpec((tm, d), lambda i, j: (i, 0)),
            pl.BlockSpec((1, d), lambda i, j: (0, 0)),
            pl.BlockSpec((d, tn), lambda i, j: (0, w_col(j))),
        ],
        out_specs=pl.BlockSpec((tm, tn), lambda i, j: (i, j)),
        out_shape=jax.ShapeDtypeStruct((t, n_cols), out_dtype),
        scratch_shapes=[pltpu.VMEM((tm, d), BF16)],
        compiler_params=pltpu.CompilerParams(
            dimension_semantics=("parallel", "arbitrary"),
            vmem_limit_bytes=_vmem_limit(blocks)),
        name=name,
    )(x, gain.reshape(1, d), w)


def _diff_attn_kernel(slope_ref, lam_ref, sg_ref, q_ref, k_ref, v_ref, o_ref,
                      m_ref, l_ref, acc_ref, *, tq, tr, tk, lambda_init):
    h = pl.program_id(1)
    qi = pl.program_id(2)
    d = DA_HEAD_DIM
    hd = 2 * d
    nr = tq // tr
    scale2 = d ** -0.5 * LOG2E
    slope2 = slope_ref[h] * LOG2E

    m_ref[...] = jnp.full_like(m_ref, NEG_BIG)
    l_ref[...] = jnp.zeros_like(l_ref)
    acc_ref[...] = jnp.zeros_like(acc_ref)

    def lanes(x, width):
        return jnp.concatenate([x] * (width // V7X_LANES), axis=1)

    def tiles(row_tiles, k_start, width, masked_r):
        kb = k_ref[0, pl.ds(k_start, width), :]
        vb = v_ref[0, pl.ds(k_start, width), :]
        col = lax.broadcasted_iota(jnp.int32, (1, width), 1)
        chains = [(r, c) for r in row_tiles for c in range(2)]
        s_all = {}
        for r, c in chains:
            bias = slope2 * (col + (k_start - (qi * tq + r * tr))).astype(F32)
            q = q_ref[0, r * tr:(r + 1) * tr, c * d:(c + 1) * d]
            s = _dot_nt(q, kb[:, c * d:(c + 1) * d]) * scale2 + bias
            if r == masked_r:
                keep = (lax.broadcasted_iota(jnp.int32, (tr, width), 0)
                        >= lax.broadcasted_iota(jnp.int32, (tr, width), 1))
                s = jnp.where(keep, s, NEG_BIG)
            s_all[r, c] = s
        p_all, alpha_all = {}, {}
        for r, c in chains:
            idx = 2 * r + c
            s = s_all[r, c]
            m_old = m_ref[idx]
            m_new = jnp.maximum(m_old, jnp.max(s, axis=-1, keepdims=True))
            alpha = jnp.exp2(m_old - m_new)
            p = jnp.exp2(s - lanes(m_new, width))
            l_ref[idx] = alpha * l_ref[idx] + jnp.sum(p, axis=-1, keepdims=True)
            m_ref[idx] = m_new
            p_all[r, c] = p.astype(BF16)
            alpha_all[r, c] = alpha
        for r, c in chains:
            idx = 2 * r + c
            acc_ref[idx] = lanes(alpha_all[r, c], hd) * acc_ref[idx] + jnp.dot(
                p_all[r, c], vb, preferred_element_type=F32)

    def body(ki, carry):
        tiles(range(nr), pl.multiple_of(ki * tk, tk), tk, None)
        return carry

    lax.fori_loop(0, qi * (tq // tk), body, 0)
    for cc in range(nr):
        tiles(range(cc, nr), pl.multiple_of(qi * tq + cc * tr, tr), tr, cc)

    lam4 = lam_ref[...]
    lam = (jnp.exp(jnp.sum(lam4[0:1] * lam4[1:2], axis=-1, keepdims=True))
           - jnp.exp(jnp.sum(lam4[2:3] * lam4[3:4], axis=-1, keepdims=True)) + lambda_init)
    for r in range(nr):
        o = (acc_ref[2 * r] / lanes(l_ref[2 * r], hd)
             - lam * (acc_ref[2 * r + 1] / lanes(l_ref[2 * r + 1], hd)))
        o = _rms(o, sg_ref[...]) * (1.0 - lambda_init)
        o_ref[0, r * tr:(r + 1) * tr, :] = o.astype(o_ref.dtype)


def _diff_attn(proj, slopes, lam4, subln_g, *, lambda_init, tq=2048, tr=512, tk=512):
    b, s, _ = proj.shape
    hd = 2 * DA_HEAD_DIM
    nchain = 2 * (tq // tr)
    blocks = (2 * tq * hd * 2 + 2 * 2 * s * hd * 2 + 2 * tq * hd * 2
              + nchain * tr * (hd + 2 * V7X_LANES) * 4 + 4 * nchain * tr * tk * 4)
    return pl.pallas_call(
        functools.partial(_diff_attn_kernel, tq=tq, tr=tr, tk=tk, lambda_init=lambda_init),
        grid=(b, DA_HEADS, s // tq),
        in_specs=[
            pl.BlockSpec(memory_space=pltpu.SMEM),
            pl.BlockSpec((4, DA_HEAD_DIM), lambda bi, h, i: (0, 0)),
            pl.BlockSpec((1, hd), lambda bi, h, i: (0, 0)),
            pl.BlockSpec((1, tq, hd), lambda bi, h, i: (bi, i, h)),
            pl.BlockSpec((1, s, hd), lambda bi, h, i: (bi, 0, DA_HEADS + h)),
            pl.BlockSpec((1, s, hd), lambda bi, h, i: (bi, 0, 2 * DA_HEADS + h)),
        ],
        out_specs=pl.BlockSpec((1, tq, hd), lambda bi, h, i: (bi, i, h)),
        out_shape=jax.ShapeDtypeStruct((b, s, DA_HEADS * hd), BF16),
        scratch_shapes=[pltpu.VMEM((nchain, tr, V7X_LANES), F32),
                        pltpu.VMEM((nchain, tr, V7X_LANES), F32),
                        pltpu.VMEM((nchain, tr, hd), F32)],
        compiler_params=pltpu.CompilerParams(
            dimension_semantics=("parallel", "parallel", "arbitrary"),
            vmem_limit_bytes=_vmem_limit(blocks)),
        name="diff_attn",
    )(slopes, lam4, subln_g.reshape(1, hd), proj, proj, proj)


def _xattn_kernel(q_ref, kv_ref, o_ref):
    hd = XA_HEAD_DIM
    for h in range(XA_HEADS):
        q = q_ref[0, :, h * hd:(h + 1) * hd]
        k = kv_ref[0, :, h * hd:(h + 1) * hd]
        v = kv_ref[0, :, (XA_HEADS + h) * hd:(XA_HEADS + h + 1) * hd]
        s = _dot_nt(q, k) * (hd ** -0.5)
        p = jnp.exp(s - jnp.max(s, axis=-1, keepdims=True))
        p = p / jnp.sum(p, axis=-1, keepdims=True)
        o_ref[0, :, h * hd:(h + 1) * hd] = jnp.dot(
            p.astype(BF16), v, preferred_element_type=F32).astype(o_ref.dtype)


def _xattn(proj, kv, *, q_col_block, tq=1024):
    b, s, _ = proj.shape
    m = kv.shape[1]
    w = XA_HEADS * XA_HEAD_DIM
    blocks = 2 * 2 * tq * w * 2 + 2 * m * 2 * w * 2 + 4 * 4 * tq * m * 4
    return pl.pallas_call(
        _xattn_kernel,
        grid=(b, s // tq),
        in_specs=[
            pl.BlockSpec((1, tq, w), lambda bi, i: (bi, i, q_col_block)),
            pl.BlockSpec((1, m, 2 * w), lambda bi, i: (bi, 0, 0)),
        ],
        out_specs=pl.BlockSpec((1, tq, w), lambda bi, i: (bi, i, 0)),
        out_shape=jax.ShapeDtypeStruct((b, s, w), BF16),
        compiler_params=pltpu.CompilerParams(
            dimension_semantics=("parallel", "parallel"),
            vmem_limit_bytes=_vmem_limit(blocks)),
        name="mem_xattn",
    )(proj, kv)


HG_CHUNK = 128


def _hgrn_levels(c):
    levels, m = [], 1
    while m < c:
        levels.append(m)
        m *= 2
    return levels


def _hgrn_constants(c):
    t = np.arange(c)[:, None]
    s = np.arange(c)[None, :]
    intervals, masks = [], []
    for m in _hgrn_levels(c):
        edge = (t // (2 * m)) * (2 * m) + m - 1
        upper = (t % (2 * m)) >= m
        intervals.append(np.where(upper, (s > edge) & (s <= t), (s > t) & (s <= edge)))
        masks.append(upper & ((s // (2 * m)) == (t // (2 * m))) & ((s % (2 * m)) < m))
    intervals.append(s <= t)
    intervals.append(s > t)
    iv = np.concatenate(intervals, 0)
    return (jnp.asarray(np.concatenate([iv, iv], 1), BF16),
            jnp.asarray(np.concatenate(masks, 0), F32))


def _hgrn_kernel(lbl_ref, ng_ref, iv_ref, mk_ref, q_ref, f_ref, i_ref, g_ref, o_ref, st_ref, *,
                 ts, nh, layer):
    c = HG_CHUNK
    kd = HG_DIM
    levels = _hgrn_levels(c)
    nl = len(levels)

    @pl.when(pl.program_id(2) == 0)
    def _():
        st_ref[...] = jnp.zeros_like(st_ref)

    logits = [lbl_ref[r, 0] for r in range(lbl_ref.shape[0])]
    mx = functools.reduce(jnp.maximum, logits)
    ex = [jnp.exp(v - mx) for v in logits]
    lb = sum(ex[1:layer + 1], ex[0]) / sum(ex[1:], ex[0])
    ng = ng_ref[0]

    def head(x, hd):
        return x[:, hd * kd:(hd + 1) * kd]

    nc = ts // c
    rows = [slice(ci * c, (ci + 1) * c) for ci in range(nc)]

    qf, kk, v16, dec, a_tot = {}, {}, {}, {}, {}

    def stage1(ci):
        sl = rows[ci]
        z = f_ref[0, sl, :]
        qq = q_ref[0, sl, :]
        sig = jax.nn.sigmoid(z)
        lf2 = jnp.log(lb + (1.0 - lb) * sig) * LOG2E
        kk[ci] = (1.0 - lb) * (1.0 - sig)
        qf[ci] = qq * jax.nn.sigmoid(qq)
        v16[ci] = i_ref[0, sl, :].astype(BF16)
        hi = lf2.astype(BF16)
        lo = (lf2 - hi.astype(F32)).astype(BF16)
        parts = jnp.concatenate([hi, lo], axis=0)
        dec[ci] = [jnp.exp2(jnp.dot(iv_ref[p * c:(p + 1) * c, :], parts,
                                    preferred_element_type=F32)) for p in range(nl + 2)]

    def stage2(ci):
        acc = [jnp.zeros((c, c), F32) for _ in range(nh)]
        for li in range(nl):
            qt = (qf[ci] * dec[ci][li]).astype(BF16)
            kt = (kk[ci] * dec[ci][li]).astype(BF16)
            mask = mk_ref[li * c:(li + 1) * c, :]
            for hd in range(nh):
                acc[hd] = acc[hd] + _dot_nt(head(qt, hd), head(kt, hd)) * mask
        a_tot[ci] = [a.astype(BF16) for a in acc]

    st = [st_ref[hd] for hd in range(nh)]

    def stage3(ci):
        sl = rows[ci]
        e_b = dec[ci][nl]
        e_r = dec[ci][nl + 1]
        qb = (qf[ci] * e_b).astype(BF16)
        kr = (kk[ci] * e_r).astype(BF16)
        qk = qf[ci] * kk[ci]
        v = i_ref[0, sl, :]
        outs = []
        for hd in range(nh):
            vh = head(v16[ci], hd)
            o = jnp.sum(head(qk, hd), axis=-1, keepdims=True) * head(v, hd)
            o = o + jnp.dot(a_tot[ci][hd], vh, preferred_element_type=F32)
            o = o + _dot_nt(head(qb, hd), st[hd].astype(BF16))
            st[hd] = st[hd] * head(e_b, hd)[c - 1:c, :] + lax.dot_general(
                vh, head(kr, hd), (((0,), (0,)), ((), ())), preferred_element_type=F32)
            outs.append(o * lax.rsqrt(jnp.mean(o * o, axis=-1, keepdims=True) + EPS))
        o = jnp.concatenate(outs, axis=1) * ng * jax.nn.sigmoid(g_ref[0, sl, :])
        o_ref[0, sl, :] = o.astype(o_ref.dtype)

    for step in range(nc + 2):
        if step < nc:
            stage1(step)
        if 0 <= step - 1 < nc:
            stage2(step - 1)
        if 0 <= step - 2 < nc:
            stage3(step - 2)
    for hd in range(nh):
        st_ref[hd] = st[hd]


def _hgrn(proj, lb_logits, norm_g, *, layer, ts=1024, nh=2):
    b, s, _ = proj.shape
    kd = HG_DIM
    c = HG_CHUNK
    w = nh * kd
    ng = HG_HEADS // nh
    nl = lb_logits.shape[0]
    intervals, masks = _hgrn_constants(c)
    blocks = (2 * 4 * ts * w * 4 + 2 * ts * w * 2 + nh * kd * kd * 4 + 2 * intervals.size * 2
              + 2 * masks.size * 4 + 96 * c * w * 4)

    def col(off):
        return lambda bi, h, i: (bi, i, off * ng + h)

    return pl.pallas_call(
        functools.partial(_hgrn_kernel, ts=ts, nh=nh, layer=layer),
        grid=(b, ng, s // ts),
        in_specs=[
            pl.BlockSpec((nl, 1, 1, w), lambda bi, h, i: (0, h, 0, 0)),
            pl.BlockSpec((1, 1, w), lambda bi, h, i: (h, 0, 0)),
            pl.BlockSpec(intervals.shape, lambda bi, h, i: (0, 0)),
            pl.BlockSpec(masks.shape, lambda bi, h, i: (0, 0)),
            pl.BlockSpec((1, ts, w), col(0)),
            pl.BlockSpec((1, ts, w), col(1)),
            pl.BlockSpec((1, ts, w), col(2)),
            pl.BlockSpec((1, ts, w), col(3)),
        ],
        out_specs=pl.BlockSpec((1, ts, w), lambda bi, h, i: (bi, i, h)),
        out_shape=jax.ShapeDtypeStruct((b, s, HG_HEADS * kd), BF16),
        scratch_shapes=[pltpu.VMEM((nh, kd, kd), F32)],
        compiler_params=pltpu.CompilerParams(
            dimension_semantics=("parallel", "parallel", "arbitrary"),
            vmem_limit_bytes=_vmem_limit(blocks)),
        name="hgrn2",
    )(lb_logits.reshape(nl, ng, 1, w), norm_g.reshape(ng, 1, w), intervals, masks,
      proj, proj, proj, proj)


def _merge_kernel(h_hbm, g_ref, y0_ref, y1_ref, y2_ref, wg0_ref, wg1_ref, wg2_ref,
                  wb0_ref, wb1_ref, wb2_ref, wout_ref, o_ref, h_buf, u_ref, mg_ref, h_sem):
    i = pl.program_id(0)
    j = pl.program_id(1)
    nj = mg_ref.shape[0]
    tn = mg_ref.shape[2]
    tm = h_buf.shape[0]

    def h_copy(tile):
        return pltpu.make_async_copy(h_hbm.at[pl.ds(tile * tm, tm), :], h_buf, h_sem)

    @pl.when((j == 0) & (i == 0))
    def _():
        h_copy(0).start()

    @pl.when(j == 0)
    def _():
        h_copy(i).wait()
        u_ref[...] = _rms(h_buf[...], g_ref[...]).astype(BF16)
        o_ref[...] = h_buf[...]

    @pl.when((j == 1) & (i + 1 < pl.num_programs(0)))
    def _():
        h_copy(i + 1).start()

    u = u_ref[...]
    merged = None
    for y_ref, wg_ref, wb_ref in ((y0_ref, wg0_ref, wb0_ref), (y1_ref, wg1_ref, wb1_ref),
                                  (y2_ref, wg2_ref, wb2_ref)):
        gate = jax.nn.sigmoid(jnp.dot(u, wg_ref[...], preferred_element_type=F32))
        term = gate * jnp.dot(y_ref[...], wb_ref[...], preferred_element_type=F32)
        merged = term if merged is None else merged + term
    mg_ref[j] = merged.astype(BF16)

    @pl.when(j == nj - 1)
    def _():
        acc = o_ref[...]
        for jj in range(nj):
            acc = acc + jnp.dot(mg_ref[jj], wout_ref[jj * tn:(jj + 1) * tn, :],
                                preferred_element_type=F32)
        o_ref[...] = acc


def _merge(h, gain, y0, y1, y2, w_in, gate_col0, wb0, wb1, wb2, wout, *, tm=512, tn=512):
    t, d = h.shape
    w = y0.shape[1]
    nj = d // tn
    assert nj >= 2
    blocks = (3 * tm * d * 4 + tm * d * 2 + tm * d * 2 + 2 * 3 * tm * w * 2
              + 2 * 3 * d * tn * 2 + 2 * 3 * w * tn * 2 + d * d * 2 + 8 * tm * tn * 4)

    def gate_spec(br):
        off = (gate_col0 + br * d) // tn
        return pl.BlockSpec((d, tn), lambda i, j: (0, off + j))

    y_spec = pl.BlockSpec((tm, w), lambda i, j: (i, 0))
    wb_spec = pl.BlockSpec((w, tn), lambda i, j: (0, j))
    return pl.pallas_call(
        _merge_kernel,
        grid=(t // tm, nj),
        in_specs=[
            pl.BlockSpec(memory_space=pl.ANY),
            pl.BlockSpec((1, d), lambda i, j: (0, 0)),
            y_spec, y_spec, y_spec,
            gate_spec(0), gate_spec(1), gate_spec(2),
            wb_spec, wb_spec, wb_spec,
            pl.BlockSpec((d, d), lambda i, j: (0, 0), pipeline_mode=pl.Buffered(1)),
        ],
        out_specs=pl.BlockSpec((tm, d), lambda i, j: (i, 0)),
        out_shape=jax.ShapeDtypeStruct((t, d), F32),
        scratch_shapes=[pltpu.VMEM((tm, d), F32), pltpu.VMEM((tm, d), BF16),
                        pltpu.VMEM((nj, tm, tn), BF16), pltpu.SemaphoreType.DMA(())],
        compiler_params=pltpu.CompilerParams(
            dimension_semantics=("arbitrary", "arbitrary"),
            vmem_limit_bytes=_vmem_limit(blocks)),
        name="merge_out",
    )(h, gain.reshape(1, d), y0, y1, y2, w_in, w_in, w_in, wb0, wb1, wb2, wout)


def kernel(x, mem, ffn1_norm, ffn1_w_gate, ffn1_w_up, ffn1_w_down, mix_norm, mem_norm, w_in,
           da_lambda_q1, da_lambda_k1, da_lambda_q2, da_lambda_k2, da_subln, hg_lb_logits, hg_norm,
           w_mem_kv, w_branch_da, w_branch_hg, w_branch_xa, w_out,
           ffn2_norm, ffn2_w_gate, ffn2_w_up, ffn2_w_down, final_norm):
    bsz, seq, d = x.shape
    depth = w_in.shape[0]
    t = bsz * seq
    da_w = DA_HEADS * 2 * DA_HEAD_DIM
    hg_w = HG_HEADS * HG_DIM
    xa_w = XA_HEADS * XA_HEAD_DIM
    slopes = jnp.asarray([2.0 ** (-8.0 * (i + 1) / DA_HEADS) for i in range(DA_HEADS)], F32)

    h = x.reshape(t, d)
    for l in range(depth):
        lambda_init = 0.8 - 0.6 * math.exp(-0.3 * l)
        h = _ffn(h, ffn1_norm[l], ffn1_w_gate[l].astype(BF16), ffn1_w_up[l].astype(BF16),
                 (0.5 * ffn1_w_down[l]).astype(BF16), final_norm, final_norm=False)

        w_in16 = w_in[l].astype(BF16)
        da_b, hg_b = 3 * da_w // PROJ_TN, 4 * hg_w // PROJ_TN
        p_att = _norm_matmul(h, mix_norm[l], w_in16, BF16, n_cols=3 * da_w + xa_w,
                             w_col=lambda j: jnp.where(j < da_b, j, j + hg_b), tm=1024,
                             tn=PROJ_TN, name="in_proj_att").reshape(bsz, seq, -1)
        p_hg = _norm_matmul(h, mix_norm[l], w_in16, F32, n_cols=4 * hg_w,
                            w_col=lambda j: j + da_b, tm=1024, tn=PROJ_TN,
                            name="in_proj_hg").reshape(bsz, seq, -1)

        lam4 = jnp.stack([da_lambda_q1[l], da_lambda_k1[l], da_lambda_q2[l], da_lambda_k2[l]])
        y_da = _diff_attn(p_att, slopes, lam4, da_subln[l], lambda_init=lambda_init)
        y_hg = _hgrn(p_hg, hg_lb_logits, hg_norm[l], layer=l)
        kv = _norm_matmul(mem.reshape(-1, d), mem_norm[l], w_mem_kv[l].astype(BF16), BF16,
                          n_cols=2 * xa_w, tm=512, tn=PROJ_TN,
                          name="mem_kv").reshape(bsz, mem.shape[1], -1)
        y_xa = _xattn(p_att, kv, q_col_block=3 * da_w // xa_w)

        h = _merge(h, mix_norm[l], y_da.reshape(t, da_w), y_hg.reshape(t, hg_w),
                   y_xa.reshape(t, xa_w), w_in16, 3 * da_w + 4 * hg_w + xa_w,
                   w_branch_da[l].astype(BF16), w_branch_hg[l].astype(BF16),
                   w_branch_xa[l].astype(BF16), w_out[l].astype(BF16))

        h = _ffn(h, ffn2_norm[l], ffn2_w_gate[l].astype(BF16), ffn2_w_up[l].astype(BF16),
                 (0.5 * ffn2_w_down[l]).astype(BF16), final_norm, final_norm=(l == depth - 1))
    return h.reshape(bsz, seq, d)
```

```python
import functools
import math

import numpy as np
import jax
import jax.numpy as jnp
from jax import lax
from jax.experimental import pallas as pl
from jax.experimental.pallas import tpu as pltpu

F32 = jnp.float32
BF16 = jnp.bfloat16

EPS = 1e-6
DA_HEADS = 4
DA_HEAD_DIM = 128
HG_HEADS = 8
HG_DIM = 128
XA_HEADS = 4
XA_HEAD_DIM = 256
N_BRANCH = 3
LOG2E = math.log2(math.e)
PROJ_TN = 1024

V7X_VMEM_BYTES = 64 * 1024 * 1024
V7X_LANES = 128

NEG_BIG = -1e30


def _vmem_limit(block_bytes):
    want = int(block_bytes * 1.25) + (4 << 20)
    return min(want, V7X_VMEM_BYTES - (4 << 20))


def _rms(x, g):
    return x * lax.rsqrt(jnp.mean(x * x, axis=-1, keepdims=True) + EPS) * g


def _dot_nt(a, b):
    return lax.dot_general(a, b, (((1,), (1,)), ((), ())), preferred_element_type=F32)


def _ffn_kernel(x_hbm, g_ref, wg_ref, wu_ref, wd_ref, fg_ref, o_ref, x_buf, xn_ref, x_sem, *,
                final_norm):
    i = pl.program_id(0)
    j = pl.program_id(1)
    tm = x_buf.shape[0]

    def x_copy(tile):
        return pltpu.make_async_copy(x_hbm.at[pl.ds(tile * tm, tm), :], x_buf, x_sem)

    @pl.when((j == 0) & (i == 0))
    def _():
        x_copy(0).start()

    @pl.when(j == 0)
    def _():
        x_copy(i).wait()
        xn_ref[...] = _rms(x_buf[...], g_ref[...]).astype(BF16)
        o_ref[...] = x_buf[...]

    @pl.when((j == 1) & (i + 1 < pl.num_programs(0)))
    def _():
        x_copy(i + 1).start()

    xn = xn_ref[...]
    a = jnp.dot(xn, wg_ref[...], preferred_element_type=F32)
    b = jnp.dot(xn, wu_ref[...], preferred_element_type=F32)
    hmid = (a * jax.nn.sigmoid(a) * b).astype(BF16)
    o_ref[...] += jnp.dot(hmid, wd_ref[...], preferred_element_type=F32)

    if final_norm:
        @pl.when(j == pl.num_programs(1) - 1)
        def _():
            o_ref[...] = _rms(o_ref[...], fg_ref[...])


def _ffn(x, gain, wg, wu, wd_half, final_gain, *, final_norm, tm=1024, tf=512):
    wd = wd_half
    t, d = x.shape
    dff = wg.shape[1]
    assert dff // tf >= 2
    blocks = 3 * tm * d * 4 + tm * d * 2 + 2 * 3 * d * tf * 2 + 3 * tm * tf * 4
    return pl.pallas_call(
        functools.partial(_ffn_kernel, final_norm=final_norm),
        grid=(t // tm, dff // tf),
        in_specs=[
            pl.BlockSpec(memory_space=pl.ANY),
            pl.BlockSpec((1, d), lambda i, j: (0, 0)),
            pl.BlockSpec((d, tf), lambda i, j: (0, j)),
            pl.BlockSpec((d, tf), lambda i, j: (0, j)),
            pl.BlockSpec((tf, d), lambda i, j: (j, 0)),
            pl.BlockSpec((1, d), lambda i, j: (0, 0)),
        ],
        out_specs=pl.BlockSpec((tm, d), lambda i, j: (i, 0)),
        out_shape=jax.ShapeDtypeStruct((t, d), F32),
        scratch_shapes=[pltpu.VMEM((tm, d), F32), pltpu.VMEM((tm, d), BF16),
                        pltpu.SemaphoreType.DMA(())],
        compiler_params=pltpu.CompilerParams(
            dimension_semantics=("arbitrary", "arbitrary"),
            vmem_limit_bytes=_vmem_limit(blocks)),
        name="ffn_final" if final_norm else "ffn",
    )(x, gain.reshape(1, d), wg, wu, wd, final_gain.reshape(1, d))


def _norm_matmul_kernel(x_ref, g_ref, w_ref, o_ref, xn_ref):
    @pl.when(pl.program_id(1) == 0)
    def _():
        xn_ref[...] = _rms(x_ref[...], g_ref[...]).astype(BF16)

    o_ref[...] = jnp.dot(xn_ref[...], w_ref[...], preferred_element_type=F32).astype(o_ref.dtype)


def _norm_matmul(x, gain, w, out_dtype, *, tm, tn=PROJ_TN, name):
    t, d = x.shape
    n_cols = w.shape[1]
    out_bytes = jnp.dtype(out_dtype).itemsize
    blocks = (2 * tm * d * 4 + tm * d * 2 + 2 * d * tn * 2 + 2 * tm * tn * out_bytes
              + tm * tn * 4)
    return pl.pallas_call(
        _norm_matmul_kernel,
        grid=(t // tm, n_cols // tn),
        in_specs=[
            pl.BlockSpec((tm, d), lambda i, j: (i, 0)),
            pl.BlockSpec((1, d), lambda i, j: (0, 0)),
            pl.BlockSpec((d, tn), lambda i, j: (0, j)),
        ],
        out_specs=pl.BlockSpec((tm, tn), lambda i, j: (i, j)),
        out_shape=jax.ShapeDtypeStruct((t, n_cols), out_dtype),
        scratch_shapes=[pltpu.VMEM((tm, d), BF16)],
        compiler_params=pltpu.CompilerParams(
            dimension_semantics=("parallel", "arbitrary"),
            vmem_limit_bytes=_vmem_limit(blocks)),
        name=name,
    )(x, gain.reshape(1, d), w)


def _in_proj_kernel(x_ref, g_ref, w_ref, att_ref, hg_ref, xn_ref, *, n_hg):
    j = pl.program_id(1)

    @pl.when(j == 0)
    def _():
        xn_ref[...] = _rms(x_ref[...], g_ref[...]).astype(BF16)

    res = jnp.dot(xn_ref[...], w_ref[...], preferred_element_type=F32)
    att_ref[...] = res.astype(att_ref.dtype)

    @pl.when(j < n_hg)
    def _():
        hg_ref[...] = res


def _in_proj(x, gain, w_in, *, da_cols, hg_cols, xa_cols, tm=1024, tn=PROJ_TN):
    t, d = x.shape
    da_b, hg_b, xa_b = da_cols // tn, hg_cols // tn, xa_cols // tn

    def w_col(j):
        a = j - hg_b
        return jnp.where(j < hg_b, da_b + j, jnp.where(a < da_b, a, a + hg_b))

    blocks = (2 * tm * d * 4 + tm * d * 2 + 2 * d * tn * 2 + 2 * tm * tn * 2 + 2 * tm * tn * 4
              + tm * tn * 4)
    return pl.pallas_call(
        functools.partial(_in_proj_kernel, n_hg=hg_b),
        grid=(t // tm, hg_b + da_b + xa_b),
        in_specs=[
            pl.BlockSpec((tm, d), lambda i, j: (i, 0)),
            pl.BlockSpec((1, d), lambda i, j: (0, 0)),
            pl.BlockSpec((d, tn), lambda i, j: (0, w_col(j))),
        ],
        out_specs=[
            pl.BlockSpec((tm, tn), lambda i, j: (i, jnp.maximum(j - hg_b, 0))),
            pl.BlockSpec((tm, tn), lambda i, j: (i, jnp.minimum(j, hg_b - 1))),
        ],
        out_shape=[jax.ShapeDtypeStruct((t, da_cols + xa_cols), BF16),
                   jax.ShapeDtypeStruct((t, hg_cols), F32)],
        scratch_shapes=[pltpu.VMEM((tm, d), BF16)],
        compiler_params=pltpu.CompilerParams(
            dimension_semantics=("parallel", "arbitrary"),
            vmem_limit_bytes=_vmem_limit(blocks)),
        name="in_proj",
    )(x, gain.reshape(1, d), w_in)


def _diff_attn_kernel(slope_ref, lam_ref, sg_ref, q_ref, k_ref, v_ref, o_ref,
                      m_ref, l_ref, acc_ref, *, tq, tr, tk, lambda_init):
    h = pl.program_id(1)
    qi = pl.program_id(2)
    d = DA_HEAD_DIM
    hd = 2 * d
    nr = tq // tr
    scale2 = d ** -0.5 * LOG2E
    slope2 = slope_ref[h] * LOG2E

    m_ref[...] = jnp.full_like(m_ref, NEG_BIG)
    l_ref[...] = jnp.zeros_like(l_ref)
    acc_ref[...] = jnp.zeros_like(acc_ref)

    def lanes(x, width):
        return jnp.concatenate([x] * (width // V7X_LANES), axis=1)

    def tiles(row_tiles, k_start, width, masked_r):
        kb = k_ref[0, pl.ds(k_start, width), :]
        vb = v_ref[0, pl.ds(k_start, width), :]
        col = lax.broadcasted_iota(jnp.int32, (1, width), 1)
        chains = [(r, c) for r in row_tiles for c in range(2)]
        s_all = {}
        for r, c in chains:
            bias = slope2 * (col + (k_start - (qi * tq + r * tr))).astype(F32)
            q = q_ref[0, r * tr:(r + 1) * tr, c * d:(c + 1) * d]
            s = _dot_nt(q, kb[:, c * d:(c + 1) * d]) * scale2 + bias
            if r == masked_r:
                keep = (lax.broadcasted_iota(jnp.int32, (tr, width), 0)
                        >= lax.broadcasted_iota(jnp.int32, (tr, width), 1))
                s = jnp.where(keep, s, NEG_BIG)
            s_all[r, c] = s
        p_all, alpha_all = {}, {}
        for r, c in chains:
            idx = 2 * r + c
            s = s_all[r, c]
            m_old = m_ref[idx]
            m_new = jnp.maximum(m_old, jnp.max(s, axis=-1, keepdims=True))
            alpha = jnp.exp2(m_old - m_new)
            p = jnp.exp2(s - lanes(m_new, width))
            l_ref[idx] = alpha * l_ref[idx] + jnp.sum(p, axis=-1, keepdims=True)
            m_ref[idx] = m_new
            p_all[r, c] = p.astype(BF16)
            alpha_all[r, c] = alpha
        for r, c in chains:
            idx = 2 * r + c
            acc_ref[idx] = lanes(alpha_all[r, c], hd) * acc_ref[idx] + jnp.dot(
                p_all[r, c], vb, preferred_element_type=F32)

    def body(ki, carry):
        tiles(range(nr), pl.multiple_of(ki * tk, tk), tk, None)
        return carry

    lax.fori_loop(0, qi * (tq // tk), body, 0)
    for cc in range(nr):
        tiles(range(cc, nr), pl.multiple_of(qi * tq + cc * tr, tr), tr, cc)

    lam4 = lam_ref[...]
    lam = (jnp.exp(jnp.sum(lam4[0:1] * lam4[1:2], axis=-1, keepdims=True))
           - jnp.exp(jnp.sum(lam4[2:3] * lam4[3:4], axis=-1, keepdims=True)) + lambda_init)
    for r in range(nr):
        o = (acc_ref[2 * r] / lanes(l_ref[2 * r], hd)
             - lam * (acc_ref[2 * r + 1] / lanes(l_ref[2 * r + 1], hd)))
        o = _rms(o, sg_ref[...]) * (1.0 - lambda_init)
        o_ref[0, r * tr:(r + 1) * tr, :] = o.astype(o_ref.dtype)


def _diff_attn(proj, slopes, lam4, subln_g, *, lambda_init, tq=2048, tr=512, tk=512):
    b, s, _ = proj.shape
    hd = 2 * DA_HEAD_DIM
    nchain = 2 * (tq // tr)
    blocks = (2 * tq * hd * 2 + 2 * 2 * s * hd * 2 + 2 * tq * hd * 2
              + nchain * tr * (hd + 2 * V7X_LANES) * 4 + 4 * nchain * tr * tk * 4)
    return pl.pallas_call(
        functools.partial(_diff_attn_kernel, tq=tq, tr=tr, tk=tk, lambda_init=lambda_init),
        grid=(b, DA_HEADS, s // tq),
        in_specs=[
            pl.BlockSpec(memory_space=pltpu.SMEM),
            pl.BlockSpec((4, DA_HEAD_DIM), lambda bi, h, i: (0, 0)),
            pl.BlockSpec((1, hd), lambda bi, h, i: (0, 0)),
            pl.BlockSpec((1, tq, hd), lambda bi, h, i: (bi, i, h)),
            pl.BlockSpec((1, s, hd), lambda bi, h, i: (bi, 0, DA_HEADS + h)),
            pl.BlockSpec((1, s, hd), lambda bi, h, i: (bi, 0, 2 * DA_HEADS + h)),
        ],
        out_specs=pl.BlockSpec((1, tq, hd), lambda bi, h, i: (bi, i, h)),
        out_shape=jax.ShapeDtypeStruct((b, s, DA_HEADS * hd), BF16),
        scratch_shapes=[pltpu.VMEM((nchain, tr, V7X_LANES), F32),
                        pltpu.VMEM((nchain, tr, V7X_LANES), F32),
                        pltpu.VMEM((nchain, tr, hd), F32)],
        compiler_params=pltpu.CompilerParams(
            dimension_semantics=("parallel", "parallel", "arbitrary"),
            vmem_limit_bytes=_vmem_limit(blocks)),
        name="diff_attn",
    )(slopes, lam4, subln_g.reshape(1, hd), proj, proj, proj)


def _xattn_kernel(q_ref, kv_ref, o_ref):
    hd = XA_HEAD_DIM
    for h in range(XA_HEADS):
        q = q_ref[0, :, h * hd:(h + 1) * hd]
        k = kv_ref[0, :, h * hd:(h + 1) * hd]
        v = kv_ref[0, :, (XA_HEADS + h) * hd:(XA_HEADS + h + 1) * hd]
        s = _dot_nt(q, k) * (hd ** -0.5)
        p = jnp.exp(s - jnp.max(s, axis=-1, keepdims=True))
        p = p / jnp.sum(p, axis=-1, keepdims=True)
        o_ref[0, :, h * hd:(h + 1) * hd] = jnp.dot(
            p.astype(BF16), v, preferred_element_type=F32).astype(o_ref.dtype)


def _xattn(proj, kv, *, q_col_block, tq=1024):
    b, s, _ = proj.shape
    m = kv.shape[1]
    w = XA_HEADS * XA_HEAD_DIM
    blocks = 2 * 2 * tq * w * 2 + 2 * m * 2 * w * 2 + 4 * 4 * tq * m * 4
    return pl.pallas_call(
        _xattn_kernel,
        grid=(b, s // tq),
        in_specs=[
            pl.BlockSpec((1, tq, w), lambda bi, i: (bi, i, q_col_block)),
            pl.BlockSpec((1, m, 2 * w), lambda bi, i: (bi, 0, 0)),
        ],
        out_specs=pl.BlockSpec((1, tq, w), lambda bi, i: (bi, i, 0)),
        out_shape=jax.ShapeDtypeStruct((b, s, w), BF16),
        compiler_params=pltpu.CompilerParams(
            dimension_semantics=("parallel", "parallel"),
            vmem_limit_bytes=_vmem_limit(blocks)),
        name="mem_xattn",
    )(proj, kv)


HG_CHUNK = 128


def _hgrn_levels(c):
    levels, m = [], 1
    while m < c:
        levels.append(m)
        m *= 2
    return levels


def _hgrn_constants(c):
    t = np.arange(c)[:, None]
    s = np.arange(c)[None, :]
    intervals, masks = [], []
    for m in _hgrn_levels(c):
        edge = (t // (2 * m)) * (2 * m) + m - 1
        upper = (t % (2 * m)) >= m
        intervals.append(np.where(upper, (s > edge) & (s <= t), (s > t) & (s <= edge)))
        masks.append(upper & ((s // (2 * m)) == (t // (2 * m))) & ((s % (2 * m)) < m))
    intervals.append(s <= t)
    intervals.append(s > t)
    iv = np.concatenate(intervals, 0)
    return (jnp.asarray(np.concatenate([iv, iv], 1), BF16),
            jnp.asarray(np.concatenate(masks, 0), F32))


def _hgrn_kernel(lbl_ref, ng_ref, iv_ref, mk_ref, q_ref, f_ref, i_ref, g_ref, o_ref, st_ref, *,
                 ts, nh, layer):
    c = HG_CHUNK
    kd = HG_DIM
    levels = _hgrn_levels(c)
    nl = len(levels)

    @pl.when(pl.program_id(2) == 0)
    def _():
        st_ref[...] = jnp.zeros_like(st_ref)

    logits = [lbl_ref[r, 0] for r in range(lbl_ref.shape[0])]
    mx = functools.reduce(jnp.maximum, logits)
    ex = [jnp.exp(v - mx) for v in logits]
    lb = sum(ex[1:layer + 1], ex[0]) / sum(ex[1:], ex[0])
    ng = ng_ref[0]

    def head(x, hd):
        return x[:, hd * kd:(hd + 1) * kd]

    nc = ts // c
    rows = [slice(ci * c, (ci + 1) * c) for ci in range(nc)]

    qf, kk, v16, dec, a_tot = {}, {}, {}, {}, {}

    def stage1(ci):
        sl = rows[ci]
        z = f_ref[0, sl, :]
        qq = q_ref[0, sl, :]
        sig = jax.nn.sigmoid(z)
        lf2 = jnp.log(lb + (1.0 - lb) * sig) * LOG2E
        kk[ci] = (1.0 - lb) * (1.0 - sig)
        qf[ci] = qq * jax.nn.sigmoid(qq)
        v16[ci] = i_ref[0, sl, :].astype(BF16)
        hi = lf2.astype(BF16)
        lo = (lf2 - hi.astype(F32)).astype(BF16)
        parts = jnp.concatenate([hi, lo], axis=0)
        dec[ci] = [jnp.exp2(jnp.dot(iv_ref[p * c:(p + 1) * c, :], parts,
                                    preferred_element_type=F32)) for p in range(nl + 2)]

    def stage2(ci):
        acc = [jnp.zeros((c, c), F32) for _ in range(nh)]
        for li in range(nl):
            qt = (qf[ci] * dec[ci][li]).astype(BF16)
            kt = (kk[ci] * dec[ci][li]).astype(BF16)
            mask = mk_ref[li * c:(li + 1) * c, :]
            for hd in range(nh):
                acc[hd] = acc[hd] + _dot_nt(head(qt, hd), head(kt, hd)) * mask
        a_tot[ci] = [a.astype(BF16) for a in acc]

    st = [st_ref[hd] for hd in range(nh)]

    def stage3(ci):
        sl = rows[ci]
        e_b = dec[ci][nl]
        e_r = dec[ci][nl + 1]
        qb = (qf[ci] * e_b).astype(BF16)
        kr = (kk[ci] * e_r).astype(BF16)
        qk = qf[ci] * kk[ci]
        v = i_ref[0, sl, :]
        outs = []
        for hd in range(nh):
            vh = head(v16[ci], hd)
            o = jnp.sum(head(qk, hd), axis=-1, keepdims=True) * head(v, hd)
            o = o + jnp.dot(a_tot[ci][hd], vh, preferred_element_type=F32)
            o = o + _dot_nt(head(qb, hd), st[hd].astype(BF16))
            st[hd] = st[hd] * head(e_b, hd)[c - 1:c, :] + lax.dot_general(
                vh, head(kr, hd), (((0,), (0,)), ((), ())), preferred_element_type=F32)
            outs.append(o * lax.rsqrt(jnp.mean(o * o, axis=-1, keepdims=True) + EPS))
        o = jnp.concatenate(outs, axis=1) * ng * jax.nn.sigmoid(g_ref[0, sl, :])
        o_ref[0, sl, :] = o.astype(o_ref.dtype)

    for step in range(nc + 2):
        if step < nc:
            stage1(step)
        if 0 <= step - 1 < nc:
            stage2(step - 1)
        if 0 <= step - 2 < nc:
            stage3(step - 2)
    for hd in range(nh):
        st_ref[hd] = st[hd]


def _hgrn(proj, lb_logits, norm_g, *, layer, ts=1024, nh=2):
    b, s, _ = proj.shape
    kd = HG_DIM
    c = HG_CHUNK
    w = nh * kd
    ng = HG_HEADS // nh
    nl = lb_logits.shape[0]
    intervals, masks = _hgrn_constants(c)
    blocks = (2 * 4 * ts * w * 4 + 2 * ts * w * 2 + nh * kd * kd * 4 + 2 * intervals.size * 2
              + 2 * masks.size * 4 + 96 * c * w * 4)

    def col(off):
        return lambda bi, h, i: (bi, i, off * ng + h)

    return pl.pallas_call(
        functools.partial(_hgrn_kernel, ts=ts, nh=nh, layer=layer),
        grid=(b, ng, s // ts),
        in_specs=[
            pl.BlockSpec((nl, 1, 1, w), lambda bi, h, i: (0, h, 0, 0)),
            pl.BlockSpec((1, 1, w), lambda bi, h, i: (h, 0, 0)),
            pl.BlockSpec(intervals.shape, lambda bi, h, i: (0, 0)),
            pl.BlockSpec(masks.shape, lambda bi, h, i: (0, 0)),
            pl.BlockSpec((1, ts, w), col(0)),
            pl.BlockSpec((1, ts, w), col(1)),
            pl.BlockSpec((1, ts, w), col(2)),
            pl.BlockSpec((1, ts, w), col(3)),
        ],
        out_specs=pl.BlockSpec((1, ts, w), lambda bi, h, i: (bi, i, h)),
        out_shape=jax.ShapeDtypeStruct((b, s, HG_HEADS * kd), BF16),
        scratch_shapes=[pltpu.VMEM((nh, kd, kd), F32)],
        compiler_params=pltpu.CompilerParams(
            dimension_semantics=("parallel", "parallel", "arbitrary"),
            vmem_limit_bytes=_vmem_limit(blocks)),
        name="hgrn2",
    )(lb_logits.reshape(nl, ng, 1, w), norm_g.reshape(ng, 1, w), intervals, masks,
      proj, proj, proj, proj)


def _merge_kernel(h_hbm, g_ref, y0_ref, y1_ref, y2_ref, wg0_ref, wg1_ref, wg2_ref,
                  wb0_ref, wb1_ref, wb2_ref, wout_ref, o_ref, h_buf, u_ref, mg_ref, h_sem, o_sem):
    i = pl.program_id(0)
    j = pl.program_id(1)
    nj = mg_ref.shape[0]
    tn = mg_ref.shape[2]
    tm = h_buf.shape[0]

    def h_copy(tile):
        return pltpu.make_async_copy(h_hbm.at[pl.ds(tile * tm, tm), :], h_buf, h_sem)

    def o_init():
        return pltpu.make_async_copy(h_buf, o_ref, o_sem)

    @pl.when((j == 0) & (i == 0))
    def _():
        h_copy(0).start()

    @pl.when(j == 0)
    def _():
        h_copy(i).wait()
        u_ref[...] = _rms(h_buf[...], g_ref[...]).astype(BF16)
        o_init().start()

    @pl.when(j == 1)
    def _():
        o_init().wait()

    @pl.when((j == 1) & (i + 1 < pl.num_programs(0)))
    def _():
        h_copy(i + 1).start()

    u = u_ref[...]
    merged = None
    for y_ref, wg_ref, wb_ref in ((y0_ref, wg0_ref, wb0_ref), (y1_ref, wg1_ref, wb1_ref),
                                  (y2_ref, wg2_ref, wb2_ref)):
        gate = jax.nn.sigmoid(jnp.dot(u, wg_ref[...], preferred_element_type=F32))
        term = gate * jnp.dot(y_ref[...], wb_ref[...], preferred_element_type=F32)
        merged = term if merged is None else merged + term
    mg_ref[j] = merged.astype(BF16)

    @pl.when(j == nj - 1)
    def _():
        acc = o_ref[...]
        for jj in range(nj):
            acc = acc + jnp.dot(mg_ref[jj], wout_ref[jj * tn:(jj + 1) * tn, :],
                                preferred_element_type=F32)
        o_ref[...] = acc


def _merge(h, gain, y0, y1, y2, w_in, gate_col0, wb0, wb1, wb2, wout, *, tm=512, tn=512):
    t, d = h.shape
    w = y0.shape[1]
    nj = d // tn
    assert nj >= 2
    blocks = (3 * tm * d * 4 + tm * d * 2 + tm * d * 2 + 2 * 3 * tm * w * 2
              + 2 * 3 * d * tn * 2 + 2 * 3 * w * tn * 2 + d * d * 2 + 8 * tm * tn * 4)

    def gate_spec(br):
        off = (gate_col0 + br * d) // tn
        return pl.BlockSpec((d, tn), lambda i, j: (0, off + j))

    y_spec = pl.BlockSpec((tm, w), lambda i, j: (i, 0))
    wb_spec = pl.BlockSpec((w, tn), lambda i, j: (0, j))
    return pl.pallas_call(
        _merge_kernel,
        grid=(t // tm, nj),
        in_specs=[
            pl.BlockSpec(memory_space=pl.ANY),
            pl.BlockSpec((1, d), lambda i, j: (0, 0)),
            y_spec, y_spec, y_spec,
            gate_spec(0), gate_spec(1), gate_spec(2),
            wb_spec, wb_spec, wb_spec,
            pl.BlockSpec((d, d), lambda i, j: (0, 0), pipeline_mode=pl.Buffered(1)),
        ],
        out_specs=pl.BlockSpec((tm, d), lambda i, j: (i, 0)),
        out_shape=jax.ShapeDtypeStruct((t, d), F32),
        scratch_shapes=[pltpu.VMEM((tm, d), F32), pltpu.VMEM((tm, d), BF16),
                        pltpu.VMEM((nj, tm, tn), BF16), pltpu.SemaphoreType.DMA(()),
                        pltpu.SemaphoreType.DMA(())],
        compiler_params=pltpu.CompilerParams(
            dimension_semantics=("arbitrary", "arbitrary"),
            vmem_limit_bytes=_vmem_limit(blocks)),
        name="merge_out",
    )(h, gain.reshape(1, d), y0, y1, y2, w_in, w_in, w_in, wb0, wb1, wb2, wout)


def kernel(x, mem, ffn1_norm, ffn1_w_gate, ffn1_w_up, ffn1_w_down, mix_norm, mem_norm, w_in,
           da_lambda_q1, da_lambda_k1, da_lambda_q2, da_lambda_k2, da_subln, hg_lb_logits, hg_norm,
           w_mem_kv, w_branch_da, w_branch_hg, w_branch_xa, w_out,
           ffn2_norm, ffn2_w_gate, ffn2_w_up, ffn2_w_down, final_norm):
    bsz, seq, d = x.shape
    depth = w_in.shape[0]
    t = bsz * seq
    da_w = DA_HEADS * 2 * DA_HEAD_DIM
    hg_w = HG_HEADS * HG_DIM
    xa_w = XA_HEADS * XA_HEAD_DIM
    slopes = jnp.asarray([2.0 ** (-8.0 * (i + 1) / DA_HEADS) for i in range(DA_HEADS)], F32)

    h = x.reshape(t, d)
    for l in range(depth):
        lambda_init = 0.8 - 0.6 * math.exp(-0.3 * l)
        h = _ffn(h, ffn1_norm[l], ffn1_w_gate[l].astype(BF16), ffn1_w_up[l].astype(BF16),
                 (0.5 * ffn1_w_down[l]).astype(BF16), final_norm, final_norm=False)

        w_in16 = w_in[l].astype(BF16)
        p_att, p_hg = _in_proj(h, mix_norm[l], w_in16, da_cols=3 * da_w, hg_cols=4 * hg_w,
                               xa_cols=xa_w)
        p_att = p_att.reshape(bsz, seq, -1)
        p_hg = p_hg.reshape(bsz, seq, -1)

        lam4 = jnp.stack([da_lambda_q1[l], da_lambda_k1[l], da_lambda_q2[l], da_lambda_k2[l]])
        y_da = _diff_attn(p_att, slopes, lam4, da_subln[l], lambda_init=lambda_init)
        y_hg = _hgrn(p_hg, hg_lb_logits, hg_norm[l], layer=l)
        kv = _norm_matmul(mem.reshape(-1, d), mem_norm[l], w_mem_kv[l].astype(BF16), BF16,
                          tm=512, name="mem_kv").reshape(bsz, mem.shape[1], -1)
        y_xa = _xattn(p_att, kv, q_col_block=3 * da_w // xa_w)

        h = _merge(h, mix_norm[l], y_da.reshape(t, da_w), y_hg.reshape(t, hg_w),
                   y_xa.reshape(t, xa_w), w_in16, 3 * da_w + 4 * hg_w + xa_w,
                   w_branch_da[l].astype(BF16), w_branch_hg[l].astype(BF16),
                   w_branch_xa[l].astype(BF16), w_out[l].astype(BF16))

        h = _ffn(h, ffn2_norm[l], ffn2_w_gate[l].astype(BF16), ffn2_w_up[l].astype(BF16),
                 (0.5 * ffn2_w_down[l]).astype(BF16), final_norm, final_norm=(l == depth - 1))
    return h.reshape(bsz, seq, d)
```

```python
import functools
import math

import numpy as np
import jax
import jax.numpy as jnp
from jax import lax
from jax.experimental import pallas as pl
from jax.experimental.pallas import tpu as pltpu

F32 = jnp.float32
BF16 = jnp.bfloat16

EPS = 1e-6
DA_HEADS = 4
DA_HEAD_DIM = 128
HG_HEADS = 8
HG_DIM = 128
XA_HEADS = 4
XA_HEAD_DIM = 256
LOG2E = math.log2(math.e)
PROJ_TN = 1024

V7X_VMEM_BYTES = 64 * 1024 * 1024
V7X_LANES = 128

NEG_BIG = -1e30


def _vmem_limit(block_bytes):
    want = int(block_bytes * 1.25) + (4 << 20)
    return min(want, V7X_VMEM_BYTES - (4 << 20))


def _rms(x, g):
    return x * lax.rsqrt(jnp.mean(x * x, axis=-1, keepdims=True) + EPS) * g


COPY_ROWS = 64


def _copy_rows(src_ref, dst_ref):
    def rows(r, carry):
        sl = pl.ds(pl.multiple_of(r * COPY_ROWS, COPY_ROWS), COPY_ROWS)
        dst_ref[sl, :] = src_ref[sl, :]
        return carry

    lax.fori_loop(0, src_ref.shape[0] // COPY_ROWS, rows, 0)


def _dot_nt(a, b):
    return lax.dot_general(a, b, (((1,), (1,)), ((), ())), preferred_element_type=F32)


def _ffn_kernel(x_hbm, g_ref, wg_ref, wu_ref, wd_ref, fg_ref, o_ref, x_buf, xn_ref, x_sem, *,
                final_norm):
    i = pl.program_id(0)
    j = pl.program_id(1)
    tm = x_buf.shape[0]

    def x_copy(tile):
        return pltpu.make_async_copy(x_hbm.at[pl.ds(tile * tm, tm), :], x_buf, x_sem)

    @pl.when((j == 0) & (i == 0))
    def _():
        x_copy(0).start()

    @pl.when(j == 0)
    def _():
        x_copy(i).wait()
        xn_ref[...] = _rms(x_buf[...], g_ref[...]).astype(BF16)
        _copy_rows(x_buf, o_ref)

    @pl.when((j == 1) & (i + 1 < pl.num_programs(0)))
    def _():
        x_copy(i + 1).start()

    xn = xn_ref[...]
    a = jnp.dot(xn, wg_ref[...], preferred_element_type=F32)
    b = jnp.dot(xn, wu_ref[...], preferred_element_type=F32)
    hmid = (a * jax.nn.sigmoid(a) * b).astype(BF16)
    o_ref[...] += jnp.dot(hmid, wd_ref[...], preferred_element_type=F32)

    if final_norm:
        @pl.when(j == pl.num_programs(1) - 1)
        def _():
            o_ref[...] = _rms(o_ref[...], fg_ref[...])


def _ffn(x, gain, wg, wu, wd_half, final_gain, *, final_norm, tm=1024, tf=512):
    wd = wd_half
    t, d = x.shape
    dff = wg.shape[1]
    assert dff // tf >= 2
    blocks = 3 * tm * d * 4 + tm * d * 2 + 2 * 3 * d * tf * 2 + 3 * tm * tf * 4
    return pl.pallas_call(
        functools.partial(_ffn_kernel, final_norm=final_norm),
        grid=(t // tm, dff // tf),
        in_specs=[
            pl.BlockSpec(memory_space=pl.ANY),
            pl.BlockSpec((1, d), lambda i, j: (0, 0)),
            pl.BlockSpec((d, tf), lambda i, j: (0, j)),
            pl.BlockSpec((d, tf), lambda i, j: (0, j)),
            pl.BlockSpec((tf, d), lambda i, j: (j, 0)),
            pl.BlockSpec((1, d), lambda i, j: (0, 0)),
        ],
        out_specs=pl.BlockSpec((tm, d), lambda i, j: (i, 0)),
        out_shape=jax.ShapeDtypeStruct((t, d), F32),
        scratch_shapes=[pltpu.VMEM((tm, d), F32), pltpu.VMEM((tm, d), BF16),
                        pltpu.SemaphoreType.DMA(())],
        compiler_params=pltpu.CompilerParams(
            dimension_semantics=("arbitrary", "arbitrary"),
            vmem_limit_bytes=_vmem_limit(blocks)),
        name="ffn_final" if final_norm else "ffn",
    )(x, gain.reshape(1, d), wg, wu, wd, final_gain.reshape(1, d))


def _norm_matmul_kernel(x_ref, g_ref, w_ref, o_ref, xn_ref):
    @pl.when(pl.program_id(1) == 0)
    def _():
        xn_ref[...] = _rms(x_ref[...], g_ref[...]).astype(BF16)

    o_ref[...] = jnp.dot(xn_ref[...], w_ref[...], preferred_element_type=F32).astype(o_ref.dtype)


def _norm_matmul(x, gain, w, out_dtype, *, tm, tn=PROJ_TN, name):
    t, d = x.shape
    n_cols = w.shape[1]
    out_bytes = jnp.dtype(out_dtype).itemsize
    blocks = (2 * tm * d * 4 + tm * d * 2 + 2 * d * tn * 2 + 2 * tm * tn * out_bytes
              + tm * tn * 4)
    return pl.pallas_call(
        _norm_matmul_kernel,
        grid=(t // tm, n_cols // tn),
        in_specs=[
            pl.BlockSpec((tm, d), lambda i, j: (i, 0)),
            pl.BlockSpec((1, d), lambda i, j: (0, 0)),
            pl.BlockSpec((d, tn), lambda i, j: (0, j)),
        ],
        out_specs=pl.BlockSpec((tm, tn), lambda i, j: (i, j)),
        out_shape=jax.ShapeDtypeStruct((t, n_cols), out_dtype),
        scratch_shapes=[pltpu.VMEM((tm, d), BF16)],
        compiler_params=pltpu.CompilerParams(
            dimension_semantics=("parallel", "arbitrary"),
            vmem_limit_bytes=_vmem_limit(blocks)),
        name=name,
    )(x, gain.reshape(1, d), w)


def _in_proj_kernel(x_ref, g_ref, w_ref, att_ref, hg_ref, xn_ref, *, n_hg):
    j = pl.program_id(1)

    @pl.when(j == 0)
    def _():
        xn_ref[...] = _rms(x_ref[...], g_ref[...]).astype(BF16)

    res = jnp.dot(xn_ref[...], w_ref[...], preferred_element_type=F32)
    att_ref[...] = res.astype(att_ref.dtype)

    @pl.when(j < n_hg)
    def _():
        hg_ref[...] = res


def _in_proj(x, gain, w_in, *, da_cols, hg_cols, xa_cols, tm=1024, tn=PROJ_TN):
    t, d = x.shape
    da_b, hg_b, xa_b = da_cols // tn, hg_cols // tn, xa_cols // tn

    def w_col(j):
        a = j - hg_b
        return jnp.where(j < hg_b, da_b + j, jnp.where(a < da_b, a, a + hg_b))

    blocks = (2 * tm * d * 4 + tm * d * 2 + 2 * d * tn * 2 + 2 * tm * tn * 2 + 2 * tm * tn * 4
              + tm * tn * 4)
    return pl.pallas_call(
        functools.partial(_in_proj_kernel, n_hg=hg_b),
        grid=(t // tm, hg_b + da_b + xa_b),
        in_specs=[
            pl.BlockSpec((tm, d), lambda i, j: (i, 0)),
            pl.BlockSpec((1, d), lambda i, j: (0, 0)),
            pl.BlockSpec((d, tn), lambda i, j: (0, w_col(j))),
        ],
        out_specs=[
            pl.BlockSpec((tm, tn), lambda i, j: (i, jnp.maximum(j - hg_b, 0))),
            pl.BlockSpec((tm, tn), lambda i, j: (i, jnp.minimum(j, hg_b - 1))),
        ],
        out_shape=[jax.ShapeDtypeStruct((t, da_cols + xa_cols), BF16),
                   jax.ShapeDtypeStruct((t, hg_cols), F32)],
        scratch_shapes=[pltpu.VMEM((tm, d), BF16)],
        compiler_params=pltpu.CompilerParams(
            dimension_semantics=("parallel", "arbitrary"),
            vmem_limit_bytes=_vmem_limit(blocks)),
        name="in_proj",
    )(x, gain.reshape(1, d), w_in)


def _diff_attn_kernel(slope_ref, lam_ref, sg_ref, q_ref, k_ref, v_ref, o_ref,
                      m_ref, l_ref, acc_ref, *, tq, tr, tk, lambda_init):
    h = pl.program_id(1)
    qi = pl.program_id(2)
    d = DA_HEAD_DIM
    hd = 2 * d
    nr = tq // tr
    scale2 = d ** -0.5 * LOG2E
    slope2 = slope_ref[h] * LOG2E

    m_ref[...] = jnp.full_like(m_ref, NEG_BIG)
    l_ref[...] = jnp.zeros_like(l_ref)
    acc_ref[...] = jnp.zeros_like(acc_ref)

    def lanes(x, width):
        return jnp.concatenate([x] * (width // V7X_LANES), axis=1)

    def tiles(row_tiles, k_start, width, masked_r):
        kb = k_ref[0, pl.ds(k_start, width), :]
        vb = v_ref[0, pl.ds(k_start, width), :]
        col = lax.broadcasted_iota(jnp.int32, (1, width), 1)
        chains = [(r, c) for r in row_tiles for c in range(2)]
        s_all = {}
        for r, c in chains:
            bias = slope2 * (col + (k_start - (qi * tq + r * tr))).astype(F32)
            q = q_ref[0, r * tr:(r + 1) * tr, c * d:(c + 1) * d]
            s = _dot_nt(q, kb[:, c * d:(c + 1) * d]) * scale2 + bias
            if r == masked_r:
                keep = (lax.broadcasted_iota(jnp.int32, (tr, width), 0)
                        >= lax.broadcasted_iota(jnp.int32, (tr, width), 1))
                s = jnp.where(keep, s, NEG_BIG)
            s_all[r, c] = s
        p_all, alpha_all = {}, {}
        for r, c in chains:
            idx = 2 * r + c
            s = s_all[r, c]
            m_old = m_ref[idx]
            m_new = jnp.maximum(m_old, jnp.max(s, axis=-1, keepdims=True))
            alpha = jnp.exp2(m_old - m_new)
            p = jnp.exp2(s - lanes(m_new, width))
            l_ref[idx] = alpha * l_ref[idx] + jnp.sum(p, axis=-1, keepdims=True)
            m_ref[idx] = m_new
            p_all[r, c] = p.astype(BF16)
            alpha_all[r, c] = alpha
        for r, c in chains:
            idx = 2 * r + c
            acc_ref[idx] = lanes(alpha_all[r, c], hd) * acc_ref[idx] + jnp.dot(
                p_all[r, c], vb, preferred_element_type=F32)

    def body(ki, carry):
        tiles(range(nr), pl.multiple_of(ki * tk, tk), tk, None)
        return carry

    lax.fori_loop(0, qi * (tq // tk), body, 0)
    for cc in range(nr):
        tiles(range(cc, nr), pl.multiple_of(qi * tq + cc * tr, tr), tr, cc)

    lam4 = lam_ref[...]
    lam = (jnp.exp(jnp.sum(lam4[0:1] * lam4[1:2], axis=-1, keepdims=True))
           - jnp.exp(jnp.sum(lam4[2:3] * lam4[3:4], axis=-1, keepdims=True)) + lambda_init)
    for r in range(nr):
        o = (acc_ref[2 * r] / lanes(l_ref[2 * r], hd)
             - lam * (acc_ref[2 * r + 1] / lanes(l_ref[2 * r + 1], hd)))
        o = _rms(o, sg_ref[...]) * (1.0 - lambda_init)
        o_ref[0, r * tr:(r + 1) * tr, :] = o.astype(o_ref.dtype)


def _diff_attn(proj, slopes, lam4, subln_g, *, lambda_init, tq=2048, tr=512, tk=512):
    b, s, _ = proj.shape
    hd = 2 * DA_HEAD_DIM
    nchain = 2 * (tq // tr)
    blocks = (2 * tq * hd * 2 + 2 * 2 * s * hd * 2 + 2 * tq * hd * 2
              + nchain * tr * (hd + 2 * V7X_LANES) * 4 + 4 * nchain * tr * tk * 4)
    return pl.pallas_call(
        functools.partial(_diff_attn_kernel, tq=tq, tr=tr, tk=tk, lambda_init=lambda_init),
        grid=(b, DA_HEADS, s // tq),
        in_specs=[
            pl.BlockSpec(memory_space=pltpu.SMEM),
            pl.BlockSpec((4, DA_HEAD_DIM), lambda bi, h, i: (0, 0)),
            pl.BlockSpec((1, hd), lambda bi, h, i: (0, 0)),
            pl.BlockSpec((1, tq, hd), lambda bi, h, i: (bi, i, h)),
            pl.BlockSpec((1, s, hd), lambda bi, h, i: (bi, 0, DA_HEADS + h)),
            pl.BlockSpec((1, s, hd), lambda bi, h, i: (bi, 0, 2 * DA_HEADS + h)),
        ],
        out_specs=pl.BlockSpec((1, tq, hd), lambda bi, h, i: (bi, i, h)),
        out_shape=jax.ShapeDtypeStruct((b, s, DA_HEADS * hd), BF16),
        scratch_shapes=[pltpu.VMEM((nchain, tr, V7X_LANES), F32),
                        pltpu.VMEM((nchain, tr, V7X_LANES), F32),
                        pltpu.VMEM((nchain, tr, hd), F32)],
        compiler_params=pltpu.CompilerParams(
            dimension_semantics=("parallel", "parallel", "arbitrary"),
            vmem_limit_bytes=_vmem_limit(blocks)),
        name="diff_attn",
    )(slopes, lam4, subln_g.reshape(1, hd), proj, proj, proj)


def _xattn_kernel(q_ref, kv_ref, o_ref):
    hd = XA_HEAD_DIM
    for h in range(XA_HEADS):
        q = q_ref[0, :, h * hd:(h + 1) * hd]
        k = kv_ref[0, :, h * hd:(h + 1) * hd]
        v = kv_ref[0, :, (XA_HEADS + h) * hd:(XA_HEADS + h + 1) * hd]
        s = _dot_nt(q, k) * (hd ** -0.5)
        p = jnp.exp(s - jnp.max(s, axis=-1, keepdims=True))
        p = p / jnp.sum(p, axis=-1, keepdims=True)
        o_ref[0, :, h * hd:(h + 1) * hd] = jnp.dot(
            p.astype(BF16), v, preferred_element_type=F32).astype(o_ref.dtype)


def _xattn(proj, kv, *, q_col_block, tq=1024):
    b, s, _ = proj.shape
    m = kv.shape[1]
    w = XA_HEADS * XA_HEAD_DIM
    blocks = 2 * 2 * tq * w * 2 + 2 * m * 2 * w * 2 + 4 * 4 * tq * m * 4
    return pl.pallas_call(
        _xattn_kernel,
        grid=(b, s // tq),
        in_specs=[
            pl.BlockSpec((1, tq, w), lambda bi, i: (bi, i, q_col_block)),
            pl.BlockSpec((1, m, 2 * w), lambda bi, i: (bi, 0, 0)),
        ],
        out_specs=pl.BlockSpec((1, tq, w), lambda bi, i: (bi, i, 0)),
        out_shape=jax.ShapeDtypeStruct((b, s, w), BF16),
        compiler_params=pltpu.CompilerParams(
            dimension_semantics=("parallel", "parallel"),
            vmem_limit_bytes=_vmem_limit(blocks)),
        name="mem_xattn",
    )(proj, kv)


HG_CHUNK = 128


def _hgrn_levels(c):
    levels, m = [], 1
    while m < c:
        levels.append(m)
        m *= 2
    return levels


def _hgrn_constants(c):
    t = np.arange(c)[:, None]
    s = np.arange(c)[None, :]
    intervals, masks = [], []
    for m in _hgrn_levels(c):
        edge = (t // (2 * m)) * (2 * m) + m - 1
        upper = (t % (2 * m)) >= m
        intervals.append(np.where(upper, (s > edge) & (s <= t), (s > t) & (s <= edge)))
        masks.append(upper & ((s // (2 * m)) == (t // (2 * m))) & ((s % (2 * m)) < m))
    intervals.append(s <= t)
    intervals.append(s > t)
    iv = np.concatenate(intervals, 0)
    return (jnp.asarray(np.concatenate([iv, iv], 1), BF16),
            jnp.asarray(np.concatenate(masks, 0), F32))


def _hgrn_kernel(lbl_ref, ng_ref, iv_ref, mk_ref, q_ref, f_ref, i_ref, g_ref, o_ref, st_ref, *,
                 ts, nh, layer):
    c = HG_CHUNK
    kd = HG_DIM
    levels = _hgrn_levels(c)
    nl = len(levels)

    @pl.when(pl.program_id(2) == 0)
    def _():
        st_ref[...] = jnp.zeros_like(st_ref)

    logits = [lbl_ref[r, 0] for r in range(lbl_ref.shape[0])]
    mx = functools.reduce(jnp.maximum, logits)
    ex = [jnp.exp(v - mx) for v in logits]
    lb = sum(ex[1:layer + 1], ex[0]) / sum(ex[1:], ex[0])
    ng = ng_ref[0]

    def head(x, hd):
        return x[:, hd * kd:(hd + 1) * kd]

    nc = ts // c
    rows = [slice(ci * c, (ci + 1) * c) for ci in range(nc)]

    qf, kk, v16, dec, a_tot = {}, {}, {}, {}, {}

    def stage1(ci):
        sl = rows[ci]
        z = f_ref[0, sl, :]
        qq = q_ref[0, sl, :]
        sig = jax.nn.sigmoid(z)
        lf2 = jnp.log(lb + (1.0 - lb) * sig) * LOG2E
        kk[ci] = (1.0 - lb) * (1.0 - sig)
        qf[ci] = qq * jax.nn.sigmoid(qq)
        v16[ci] = i_ref[0, sl, :].astype(BF16)
        hi = lf2.astype(BF16)
        lo = (lf2 - hi.astype(F32)).astype(BF16)
        parts = jnp.concatenate([hi, lo], axis=0)
        dec[ci] = [jnp.exp2(jnp.dot(iv_ref[p * c:(p + 1) * c, :], parts,
                                    preferred_element_type=F32)) for p in range(nl + 2)]

    def stage2(ci):
        acc = [jnp.zeros((c, c), F32) for _ in range(nh)]
        for li in range(nl):
            qt = (qf[ci] * dec[ci][li]).astype(BF16)
            kt = (kk[ci] * dec[ci][li]).astype(BF16)
            mask = mk_ref[li * c:(li + 1) * c, :]
            for hd in range(nh):
                acc[hd] = acc[hd] + _dot_nt(head(qt, hd), head(kt, hd)) * mask
        a_tot[ci] = [a.astype(BF16) for a in acc]

    st = [st_ref[hd] for hd in range(nh)]

    def stage3(ci):
        sl = rows[ci]
        e_b = dec[ci][nl]
        e_r = dec[ci][nl + 1]
        qb = (qf[ci] * e_b).astype(BF16)
        kr = (kk[ci] * e_r).astype(BF16)
        qk = qf[ci] * kk[ci]
        v = i_ref[0, sl, :]
        outs = []
        for hd in range(nh):
            vh = head(v16[ci], hd)
            o = jnp.sum(head(qk, hd), axis=-1, keepdims=True) * head(v, hd)
            o = o + jnp.dot(a_tot[ci][hd], vh, preferred_element_type=F32)
            o = o + _dot_nt(head(qb, hd), st[hd].astype(BF16))
            st[hd] = st[hd] * head(e_b, hd)[c - 1:c, :] + lax.dot_general(
                vh, head(kr, hd), (((0,), (0,)), ((), ())), preferred_element_type=F32)
            outs.append(o * lax.rsqrt(jnp.mean(o * o, axis=-1, keepdims=True) + EPS))
        o = jnp.concatenate(outs, axis=1) * ng * jax.nn.sigmoid(g_ref[0, sl, :])
        o_ref[0, sl, :] = o.astype(o_ref.dtype)

    for step in range(nc + 2):
        if step < nc:
            stage1(step)
        if 0 <= step - 1 < nc:
            stage2(step - 1)
        if 0 <= step - 2 < nc:
            stage3(step - 2)
    for hd in range(nh):
        st_ref[hd] = st[hd]


def _hgrn(proj, lb_logits, norm_g, *, layer, ts=1024, nh=2):
    b, s, _ = proj.shape
    kd = HG_DIM
    c = HG_CHUNK
    w = nh * kd
    ng = HG_HEADS // nh
    nl = lb_logits.shape[0]
    intervals, masks = _hgrn_constants(c)
    blocks = (2 * 4 * ts * w * 4 + 2 * ts * w * 2 + nh * kd * kd * 4 + 2 * intervals.size * 2
              + 2 * masks.size * 4 + 96 * c * w * 4)

    def col(off):
        return lambda bi, h, i: (bi, i, off * ng + h)

    return pl.pallas_call(
        functools.partial(_hgrn_kernel, ts=ts, nh=nh, layer=layer),
        grid=(b, ng, s // ts),
        in_specs=[
            pl.BlockSpec((nl, 1, 1, w), lambda bi, h, i: (0, h, 0, 0)),
            pl.BlockSpec((1, 1, w), lambda bi, h, i: (h, 0, 0)),
            pl.BlockSpec(intervals.shape, lambda bi, h, i: (0, 0)),
            pl.BlockSpec(masks.shape, lambda bi, h, i: (0, 0)),
            pl.BlockSpec((1, ts, w), col(0)),
            pl.BlockSpec((1, ts, w), col(1)),
            pl.BlockSpec((1, ts, w), col(2)),
            pl.BlockSpec((1, ts, w), col(3)),
        ],
        out_specs=pl.BlockSpec((1, ts, w), lambda bi, h, i: (bi, i, h)),
        out_shape=jax.ShapeDtypeStruct((b, s, HG_HEADS * kd), BF16),
        scratch_shapes=[pltpu.VMEM((nh, kd, kd), F32)],
        compiler_params=pltpu.CompilerParams(
            dimension_semantics=("parallel", "parallel", "arbitrary"),
            vmem_limit_bytes=_vmem_limit(blocks)),
        name="hgrn2",
    )(lb_logits.reshape(nl, ng, 1, w), norm_g.reshape(ng, 1, w), intervals, masks,
      proj, proj, proj, proj)


def _merge_kernel(h_hbm, g_ref, y0_ref, y1_ref, y2_ref, wg0_ref, wg1_ref, wg2_ref,
                  wb0_ref, wb1_ref, wb2_ref, wout_ref, o_ref, h_buf, u_ref, mg_ref, h_sem, o_sem):
    i = pl.program_id(0)
    j = pl.program_id(1)
    nj = mg_ref.shape[0]
    tn = mg_ref.shape[2]
    tm = h_buf.shape[0]

    def h_copy(tile):
        return pltpu.make_async_copy(h_hbm.at[pl.ds(tile * tm, tm), :], h_buf, h_sem)

    def o_init():
        return pltpu.make_async_copy(h_buf, o_ref, o_sem)

    @pl.when((j == 0) & (i == 0))
    def _():
        h_copy(0).start()

    @pl.when(j == 0)
    def _():
        h_copy(i).wait()
        u_ref[...] = _rms(h_buf[...], g_ref[...]).astype(BF16)
        o_init().start()

    @pl.when(j == 1)
    def _():
        o_init().wait()

    @pl.when((j == 1) & (i + 1 < pl.num_programs(0)))
    def _():
        h_copy(i + 1).start()

    u = u_ref[...]
    merged = None
    for y_ref, wg_ref, wb_ref in ((y0_ref, wg0_ref, wb0_ref), (y1_ref, wg1_ref, wb1_ref),
                                  (y2_ref, wg2_ref, wb2_ref)):
        gate = jax.nn.sigmoid(jnp.dot(u, wg_ref[...], preferred_element_type=F32))
        term = gate * jnp.dot(y_ref[...], wb_ref[...], preferred_element_type=F32)
        merged = term if merged is None else merged + term
    mg_ref[j] = merged.astype(BF16)

    @pl.when(j == nj - 1)
    def _():
        acc = o_ref[...]
        for jj in range(nj):
            acc = acc + jnp.dot(mg_ref[jj], wout_ref[jj * tn:(jj + 1) * tn, :],
                                preferred_element_type=F32)
        o_ref[...] = acc


def _merge(h, gain, y0, y1, y2, w_in, gate_col0, wb0, wb1, wb2, wout, *, tm=512, tn=512):
    t, d = h.shape
    w = y0.shape[1]
    nj = d // tn
    assert nj >= 2
    blocks = (3 * tm * d * 4 + tm * d * 2 + tm * d * 2 + 2 * 3 * tm * w * 2
              + 2 * 3 * d * tn * 2 + 2 * 3 * w * tn * 2 + d * d * 2 + 8 * tm * tn * 4)

    def gate_spec(br):
        off = (gate_col0 + br * d) // tn
        return pl.BlockSpec((d, tn), lambda i, j: (0, off + j))

    y_spec = pl.BlockSpec((tm, w), lambda i, j: (i, 0))
    wb_spec = pl.BlockSpec((w, tn), lambda i, j: (0, j))
    return pl.pallas_call(
        _merge_kernel,
        grid=(t // tm, nj),
        in_specs=[
            pl.BlockSpec(memory_space=pl.ANY),
            pl.BlockSpec((1, d), lambda i, j: (0, 0)),
            y_spec, y_spec, y_spec,
            gate_spec(0), gate_spec(1), gate_spec(2),
            wb_spec, wb_spec, wb_spec,
            pl.BlockSpec((d, d), lambda i, j: (0, 0), pipeline_mode=pl.Buffered(1)),
        ],
        out_specs=pl.BlockSpec((tm, d), lambda i, j: (i, 0)),
        out_shape=jax.ShapeDtypeStruct((t, d), F32),
        scratch_shapes=[pltpu.VMEM((tm, d), F32), pltpu.VMEM((tm, d), BF16),
                        pltpu.VMEM((nj, tm, tn), BF16), pltpu.SemaphoreType.DMA(()),
                        pltpu.SemaphoreType.DMA(())],
        compiler_params=pltpu.CompilerParams(
            dimension_semantics=("arbitrary", "arbitrary"),
            vmem_limit_bytes=_vmem_limit(blocks)),
        name="merge_out",
    )(h, gain.reshape(1, d), y0, y1, y2, w_in, w_in, w_in, wb0, wb1, wb2, wout)


def kernel(x, mem, ffn1_norm, ffn1_w_gate, ffn1_w_up, ffn1_w_down, mix_norm, mem_norm, w_in,
           da_lambda_q1, da_lambda_k1, da_lambda_q2, da_lambda_k2, da_subln, hg_lb_logits, hg_norm,
           w_mem_kv, w_branch_da, w_branch_hg, w_branch_xa, w_out,
           ffn2_norm, ffn2_w_gate, ffn2_w_up, ffn2_w_down, final_norm):
    bsz, seq, d = x.shape
    depth = w_in.shape[0]
    t = bsz * seq
    da_w = DA_HEADS * 2 * DA_HEAD_DIM
    hg_w = HG_HEADS * HG_DIM
    xa_w = XA_HEADS * XA_HEAD_DIM
    slopes = jnp.asarray([2.0 ** (-8.0 * (i + 1) / DA_HEADS) for i in range(DA_HEADS)], F32)

    h = x.reshape(t, d)
    for l in range(depth):
        lambda_init = 0.8 - 0.6 * math.exp(-0.3 * l)
        h = _ffn(h, ffn1_norm[l], ffn1_w_gate[l].astype(BF16), ffn1_w_up[l].astype(BF16),
                 (0.5 * ffn1_w_down[l]).astype(BF16), final_norm, final_norm=False)

        w_in16 = w_in[l].astype(BF16)
        p_att, p_hg = _in_proj(h, mix_norm[l], w_in16, da_cols=3 * da_w, hg_cols=4 * hg_w,
                               xa_cols=xa_w)
        p_att = p_att.reshape(bsz, seq, -1)
        p_hg = p_hg.reshape(bsz, seq, -1)

        lam4 = jnp.stack([da_lambda_q1[l], da_lambda_k1[l], da_lambda_q2[l], da_lambda_k2[l]])
        y_da = _diff_attn(p_att, slopes, lam4, da_subln[l], lambda_init=lambda_init)
        y_hg = _hgrn(p_hg, hg_lb_logits, hg_norm[l], layer=l)
        kv = _norm_matmul(mem.reshape(-1, d), mem_norm[l], w_mem_kv[l].astype(BF16), BF16,
                          tm=512, name="mem_kv").reshape(bsz, mem.shape[1], -1)
        y_xa = _xattn(p_att, kv, q_col_block=3 * da_w // xa_w)

        h = _merge(h, mix_norm[l], y_da.reshape(t, da_w), y_hg.reshape(t, hg_w),
                   y_xa.reshape(t, xa_w), w_in16, 3 * da_w + 4 * hg_w + xa_w,
                   w_branch_da[l].astype(BF16), w_branch_hg[l].astype(BF16),
                   w_branch_xa[l].astype(BF16), w_out[l].astype(BF16))

        h = _ffn(h, ffn2_norm[l], ffn2_w_gate[l].astype(BF16), ffn2_w_up[l].astype(BF16),
                 (0.5 * ffn2_w_down[l]).astype(BF16), final_norm, final_norm=(l == depth - 1))
    return h.reshape(bsz, seq, d)
```

```python
import functools
import math

import numpy as np
import jax
import jax.numpy as jnp
from jax import lax
from jax.experimental import pallas as pl
from jax.experimental.pallas import tpu as pltpu

F32 = jnp.float32
BF16 = jnp.bfloat16

EPS = 1e-6
DA_HEADS = 4
DA_HEAD_DIM = 128
HG_HEADS = 8
HG_DIM = 128
XA_HEADS = 4
XA_HEAD_DIM = 256
LOG2E = math.log2(math.e)
PROJ_TN = 1024

V7X_VMEM_BYTES = 64 * 1024 * 1024
V7X_LANES = 128

NEG_BIG = -1e30


def _vmem_limit(block_bytes):
    want = int(block_bytes * 1.25) + (4 << 20)
    return min(want, V7X_VMEM_BYTES - (4 << 20))


def _rms(x, g):
    return x * lax.rsqrt(jnp.mean(x * x, axis=-1, keepdims=True) + EPS) * g


COPY_ROWS = 64


def _copy_rows(src_ref, dst_ref):
    def rows(r, carry):
        sl = pl.ds(pl.multiple_of(r * COPY_ROWS, COPY_ROWS), COPY_ROWS)
        dst_ref[sl, :] = src_ref[sl, :]
        return carry

    lax.fori_loop(0, src_ref.shape[0] // COPY_ROWS, rows, 0)


def _dot_nt(a, b):
    return lax.dot_general(a, b, (((1,), (1,)), ((), ())), preferred_element_type=F32)


def _ffn_kernel(x_hbm, g_ref, wg_ref, wu_ref, wd_ref, fg_ref, o_ref, x_buf, xn_ref, x_sem, *,
                final_norm):
    i = pl.program_id(0)
    j = pl.program_id(1)
    tm = x_buf.shape[0]

    def x_copy(tile):
        return pltpu.make_async_copy(x_hbm.at[pl.ds(tile * tm, tm), :], x_buf, x_sem)

    @pl.when((j == 0) & (i == 0))
    def _():
        x_copy(0).start()

    @pl.when(j == 0)
    def _():
        x_copy(i).wait()
        xn_ref[...] = _rms(x_buf[...], g_ref[...]).astype(BF16)
        _copy_rows(x_buf, o_ref)

    @pl.when((j == 1) & (i + 1 < pl.num_programs(0)))
    def _():
        x_copy(i + 1).start()

    xn = xn_ref[...]
    a = jnp.dot(xn, wg_ref[...], preferred_element_type=F32)
    b = jnp.dot(xn, wu_ref[...], preferred_element_type=F32)
    hmid = (a * jax.nn.sigmoid(a) * b).astype(BF16)
    o_ref[...] += jnp.dot(hmid, wd_ref[...], preferred_element_type=F32)

    if final_norm:
        @pl.when(j == pl.num_programs(1) - 1)
        def _():
            o_ref[...] = _rms(o_ref[...], fg_ref[...])


def _ffn(x, gain, wg, wu, wd_half, final_gain, *, final_norm, tm=1024, tf=512):
    wd = wd_half
    t, d = x.shape
    dff = wg.shape[1]
    assert dff // tf >= 2
    blocks = 3 * tm * d * 4 + tm * d * 2 + 2 * 3 * d * tf * 2 + 3 * tm * tf * 4
    return pl.pallas_call(
        functools.partial(_ffn_kernel, final_norm=final_norm),
        grid=(t // tm, dff // tf),
        in_specs=[
            pl.BlockSpec(memory_space=pl.ANY),
            pl.BlockSpec((1, d), lambda i, j: (0, 0)),
            pl.BlockSpec((d, tf), lambda i, j: (0, j)),
            pl.BlockSpec((d, tf), lambda i, j: (0, j)),
            pl.BlockSpec((tf, d), lambda i, j: (j, 0)),
            pl.BlockSpec((1, d), lambda i, j: (0, 0)),
        ],
        out_specs=pl.BlockSpec((tm, d), lambda i, j: (i, 0)),
        out_shape=jax.ShapeDtypeStruct((t, d), F32),
        scratch_shapes=[pltpu.VMEM((tm, d), F32), pltpu.VMEM((tm, d), BF16),
                        pltpu.SemaphoreType.DMA(())],
        compiler_params=pltpu.CompilerParams(
            dimension_semantics=("arbitrary", "arbitrary"),
            vmem_limit_bytes=_vmem_limit(blocks)),
        name="ffn_final" if final_norm else "ffn",
    )(x, gain.reshape(1, d), wg, wu, wd, final_gain.reshape(1, d))


def _norm_matmul_kernel(x_ref, g_ref, w_ref, o_ref, xn_ref):
    @pl.when(pl.program_id(1) == 0)
    def _():
        xn_ref[...] = _rms(x_ref[...], g_ref[...]).astype(BF16)

    o_ref[...] = jnp.dot(xn_ref[...], w_ref[...], preferred_element_type=F32).astype(o_ref.dtype)


def _norm_matmul(x, gain, w, out_dtype, *, tm, tn=PROJ_TN, name):
    t, d = x.shape
    n_cols = w.shape[1]
    out_bytes = jnp.dtype(out_dtype).itemsize
    blocks = (2 * tm * d * 4 + tm * d * 2 + 2 * d * tn * 2 + 2 * tm * tn * out_bytes
              + tm * tn * 4)
    return pl.pallas_call(
        _norm_matmul_kernel,
        grid=(t // tm, n_cols // tn),
        in_specs=[
            pl.BlockSpec((tm, d), lambda i, j: (i, 0)),
            pl.BlockSpec((1, d), lambda i, j: (0, 0)),
            pl.BlockSpec((d, tn), lambda i, j: (0, j)),
        ],
        out_specs=pl.BlockSpec((tm, tn), lambda i, j: (i, j)),
        out_shape=jax.ShapeDtypeStruct((t, n_cols), out_dtype),
        scratch_shapes=[pltpu.VMEM((tm, d), BF16)],
        compiler_params=pltpu.CompilerParams(
            dimension_semantics=("parallel", "arbitrary"),
            vmem_limit_bytes=_vmem_limit(blocks)),
        name=name,
    )(x, gain.reshape(1, d), w)


def _in_proj_kernel(x_ref, g_ref, w_ref, att_ref, hg_ref, xn_ref, *, n_hg):
    j = pl.program_id(1)

    @pl.when(j == 0)
    def _():
        xn_ref[...] = _rms(x_ref[...], g_ref[...]).astype(BF16)

    res = jnp.dot(xn_ref[...], w_ref[...], preferred_element_type=F32)
    att_ref[...] = res.astype(att_ref.dtype)

    @pl.when(j < n_hg)
    def _():
        hg_ref[...] = res


def _in_proj(x, gain, w_in, *, da_cols, hg_cols, xa_cols, tm=1024, tn=PROJ_TN):
    t, d = x.shape
    da_b, hg_b, xa_b = da_cols // tn, hg_cols // tn, xa_cols // tn

    def w_col(j):
        a = j - hg_b
        return jnp.where(j < hg_b, da_b + j, jnp.where(a < da_b, a, a + hg_b))

    blocks = (2 * tm * d * 4 + tm * d * 2 + 2 * d * tn * 2 + 2 * tm * tn * 2 + 2 * tm * tn * 4
              + tm * tn * 4)
    return pl.pallas_call(
        functools.partial(_in_proj_kernel, n_hg=hg_b),
        grid=(t // tm, hg_b + da_b + xa_b),
        in_specs=[
            pl.BlockSpec((tm, d), lambda i, j: (i, 0)),
            pl.BlockSpec((1, d), lambda i, j: (0, 0)),
            pl.BlockSpec((d, tn), lambda i, j: (0, w_col(j))),
        ],
        out_specs=[
            pl.BlockSpec((tm, tn), lambda i, j: (i, jnp.maximum(j - hg_b, 0))),
            pl.BlockSpec((tm, tn), lambda i, j: (i, jnp.minimum(j, hg_b - 1))),
        ],
        out_shape=[jax.ShapeDtypeStruct((t, da_cols + xa_cols), BF16),
                   jax.ShapeDtypeStruct((t, hg_cols), F32)],
        scratch_shapes=[pltpu.VMEM((tm, d), BF16)],
        compiler_params=pltpu.CompilerParams(
            dimension_semantics=("parallel", "arbitrary"),
            vmem_limit_bytes=_vmem_limit(blocks)),
        name="in_proj",
    )(x, gain.reshape(1, d), w_in)


BF16_SUBLANES = 16


def _hosted_specs(weights, n_steps, step_index):
    in_specs, out_specs, out_shapes, nbytes = [], [], [], 0
    for w in weights:
        rows, cols = w.shape
        slab = rows // n_steps
        assert rows % n_steps == 0 and slab % BF16_SUBLANES == 0
        spec = pl.BlockSpec((slab, cols), lambda *g: (step_index(*g), 0))
        in_specs.append(spec)
        out_specs.append(spec)
        out_shapes.append(jax.ShapeDtypeStruct(w.shape, BF16))
        nbytes += 2 * slab * cols * (4 + 2)
    return in_specs, out_specs, out_shapes, nbytes


def _cast_hosted(src_refs, dst_refs, scales):
    for src, dst, scale in zip(src_refs, dst_refs, scales):
        x = src[...]
        dst[...] = (x if scale == 1.0 else x * scale).astype(BF16)


def _diff_attn_kernel(slope_ref, lam_ref, sg_ref, q_ref, k_ref, v_ref, *refs,
                      host_scales, tq, tr, tk, lambda_init):
    nhost = len(host_scales)
    host_in, o_ref, host_out = refs[:nhost], refs[nhost], refs[nhost + 1:2 * nhost + 1]
    m_ref, l_ref, acc_ref = refs[2 * nhost + 1:]
    h = pl.program_id(1)
    qi = pl.program_id(2)
    d = DA_HEAD_DIM
    hd = 2 * d
    nr = tq // tr
    scale2 = d ** -0.5 * LOG2E
    slope2 = slope_ref[h] * LOG2E

    m_ref[...] = jnp.full_like(m_ref, NEG_BIG)
    l_ref[...] = jnp.zeros_like(l_ref)
    acc_ref[...] = jnp.zeros_like(acc_ref)

    def lanes(x, width):
        return jnp.concatenate([x] * (width // V7X_LANES), axis=1)

    def tiles(row_tiles, k_start, width, masked_r):
        kb = k_ref[0, pl.ds(k_start, width), :]
        vb = v_ref[0, pl.ds(k_start, width), :]
        col = lax.broadcasted_iota(jnp.int32, (1, width), 1)
        chains = [(r, c) for r in row_tiles for c in range(2)]
        s_all = {}
        for r, c in chains:
            bias = slope2 * (col + (k_start - (qi * tq + r * tr))).astype(F32)
            q = q_ref[0, r * tr:(r + 1) * tr, c * d:(c + 1) * d]
            s = _dot_nt(q, kb[:, c * d:(c + 1) * d]) * scale2 + bias
            if r == masked_r:
                keep = (lax.broadcasted_iota(jnp.int32, (tr, width), 0)
                        >= lax.broadcasted_iota(jnp.int32, (tr, width), 1))
                s = jnp.where(keep, s, NEG_BIG)
            s_all[r, c] = s
        p_all, alpha_all = {}, {}
        for r, c in chains:
            idx = 2 * r + c
            s = s_all[r, c]
            m_old = m_ref[idx]
            m_new = jnp.maximum(m_old, jnp.max(s, axis=-1, keepdims=True))
            alpha = jnp.exp2(m_old - m_new)
            p = jnp.exp2(s - lanes(m_new, width))
            l_ref[idx] = alpha * l_ref[idx] + jnp.sum(p, axis=-1, keepdims=True)
            m_ref[idx] = m_new
            p_all[r, c] = p.astype(BF16)
            alpha_all[r, c] = alpha
        for r, c in chains:
            idx = 2 * r + c
            acc_ref[idx] = lanes(alpha_all[r, c], hd) * acc_ref[idx] + jnp.dot(
                p_all[r, c], vb, preferred_element_type=F32)

    def body(ki, carry):
        tiles(range(nr), pl.multiple_of(ki * tk, tk), tk, None)
        return carry

    lax.fori_loop(0, qi * (tq // tk), body, 0)
    _cast_hosted(host_in, host_out, host_scales)
    for cc in range(nr):
        tiles(range(cc, nr), pl.multiple_of(qi * tq + cc * tr, tr), tr, cc)

    lam4 = lam_ref[...]
    lam = (jnp.exp(jnp.sum(lam4[0:1] * lam4[1:2], axis=-1, keepdims=True))
           - jnp.exp(jnp.sum(lam4[2:3] * lam4[3:4], axis=-1, keepdims=True)) + lambda_init)
    for r in range(nr):
        o = (acc_ref[2 * r] / lanes(l_ref[2 * r], hd)
             - lam * (acc_ref[2 * r + 1] / lanes(l_ref[2 * r + 1], hd)))
        o = _rms(o, sg_ref[...]) * (1.0 - lambda_init)
        o_ref[0, r * tr:(r + 1) * tr, :] = o.astype(o_ref.dtype)


def _diff_attn(proj, slopes, lam4, subln_g, *, lambda_init, hosted=(), host_scales=(),
               tq=2048, tr=512, tk=512):
    b, s, _ = proj.shape
    hd = 2 * DA_HEAD_DIM
    nchain = 2 * (tq // tr)
    nq = s // tq
    h_in, h_out, h_shapes, h_bytes = _hosted_specs(
        hosted, b * DA_HEADS * nq, lambda bi, h, i: (bi * DA_HEADS + h) * nq + i)
    blocks = (2 * tq * hd * 2 + 2 * 2 * s * hd * 2 + 2 * tq * hd * 2
              + nchain * tr * (hd + 2 * V7X_LANES) * 4 + 4 * nchain * tr * tk * 4 + h_bytes)
    return pl.pallas_call(
        functools.partial(_diff_attn_kernel, host_scales=tuple(host_scales), tq=tq, tr=tr, tk=tk,
                          lambda_init=lambda_init),
        grid=(b, DA_HEADS, nq),
        in_specs=[
            pl.BlockSpec(memory_space=pltpu.SMEM),
            pl.BlockSpec((4, DA_HEAD_DIM), lambda bi, h, i: (0, 0)),
            pl.BlockSpec((1, hd), lambda bi, h, i: (0, 0)),
            pl.BlockSpec((1, tq, hd), lambda bi, h, i: (bi, i, h)),
            pl.BlockSpec((1, s, hd), lambda bi, h, i: (bi, 0, DA_HEADS + h)),
            pl.BlockSpec((1, s, hd), lambda bi, h, i: (bi, 0, 2 * DA_HEADS + h)),
        ] + h_in,
        out_specs=[pl.BlockSpec((1, tq, hd), lambda bi, h, i: (bi, i, h))] + h_out,
        out_shape=[jax.ShapeDtypeStruct((b, s, DA_HEADS * hd), BF16)] + h_shapes,
        scratch_shapes=[pltpu.VMEM((nchain, tr, V7X_LANES), F32),
                        pltpu.VMEM((nchain, tr, V7X_LANES), F32),
                        pltpu.VMEM((nchain, tr, hd), F32)],
        compiler_params=pltpu.CompilerParams(
            dimension_semantics=("parallel", "parallel", "arbitrary"),
            vmem_limit_bytes=_vmem_limit(blocks)),
        name="diff_attn",
    )(slopes, lam4, subln_g.reshape(1, hd), proj, proj, proj, *hosted)


def _xattn_kernel(q_ref, kv_ref, o_ref):
    hd = XA_HEAD_DIM
    for h in range(XA_HEADS):
        q = q_ref[0, :, h * hd:(h + 1) * hd]
        k = kv_ref[0, :, h * hd:(h + 1) * hd]
        v = kv_ref[0, :, (XA_HEADS + h) * hd:(XA_HEADS + h + 1) * hd]
        s = _dot_nt(q, k) * (hd ** -0.5)
        p = jnp.exp(s - jnp.max(s, axis=-1, keepdims=True))
        p = p / jnp.sum(p, axis=-1, keepdims=True)
        o_ref[0, :, h * hd:(h + 1) * hd] = jnp.dot(
            p.astype(BF16), v, preferred_element_type=F32).astype(o_ref.dtype)


def _xattn(proj, kv, *, q_col_block, tq=1024):
    b, s, _ = proj.shape
    m = kv.shape[1]
    w = XA_HEADS * XA_HEAD_DIM
    blocks = 2 * 2 * tq * w * 2 + 2 * m * 2 * w * 2 + 4 * 4 * tq * m * 4
    return pl.pallas_call(
        _xattn_kernel,
        grid=(b, s // tq),
        in_specs=[
            pl.BlockSpec((1, tq, w), lambda bi, i: (bi, i, q_col_block)),
            pl.BlockSpec((1, m, 2 * w), lambda bi, i: (bi, 0, 0)),
        ],
        out_specs=pl.BlockSpec((1, tq, w), lambda bi, i: (bi, i, 0)),
        out_shape=jax.ShapeDtypeStruct((b, s, w), BF16),
        compiler_params=pltpu.CompilerParams(
            dimension_semantics=("parallel", "parallel"),
            vmem_limit_bytes=_vmem_limit(blocks)),
        name="mem_xattn",
    )(proj, kv)


HG_CHUNK = 128


def _hgrn_levels(c):
    levels, m = [], 1
    while m < c:
        levels.append(m)
        m *= 2
    return levels


def _hgrn_constants(c):
    t = np.arange(c)[:, None]
    s = np.arange(c)[None, :]
    intervals, masks = [], []
    for m in _hgrn_levels(c):
        edge = (t // (2 * m)) * (2 * m) + m - 1
        upper = (t % (2 * m)) >= m
        intervals.append(np.where(upper, (s > edge) & (s <= t), (s > t) & (s <= edge)))
        masks.append(upper & ((s // (2 * m)) == (t // (2 * m))) & ((s % (2 * m)) < m))
    intervals.append(s <= t)
    intervals.append(s > t)
    iv = np.concatenate(intervals, 0)
    return (jnp.asarray(np.concatenate([iv, iv], 1), BF16),
            jnp.asarray(np.concatenate(masks, 0), F32))


def _hgrn_kernel(lbl_ref, ng_ref, iv_ref, mk_ref, q_ref, f_ref, i_ref, g_ref, *refs,
                 host_scales, ts, nh, layer):
    nhost = len(host_scales)
    host_in, o_ref, host_out = refs[:nhost], refs[nhost], refs[nhost + 1:2 * nhost + 1]
    st_ref = refs[2 * nhost + 1]
    c = HG_CHUNK
    kd = HG_DIM
    levels = _hgrn_levels(c)
    nl = len(levels)

    @pl.when(pl.program_id(2) == 0)
    def _():
        st_ref[...] = jnp.zeros_like(st_ref)

    _cast_hosted(host_in, host_out, host_scales)

    logits = [lbl_ref[r, 0] for r in range(lbl_ref.shape[0])]
    mx = functools.reduce(jnp.maximum, logits)
    ex = [jnp.exp(v - mx) for v in logits]
    lb = sum(ex[1:layer + 1], ex[0]) / sum(ex[1:], ex[0])
    ng = ng_ref[0]

    def head(x, hd):
        return x[:, hd * kd:(hd + 1) * kd]

    nc = ts // c
    rows = [slice(ci * c, (ci + 1) * c) for ci in range(nc)]

    qf, kk, v16, dec, a_tot = {}, {}, {}, {}, {}

    def stage1(ci):
        sl = rows[ci]
        z = f_ref[0, sl, :]
        qq = q_ref[0, sl, :]
        sig = jax.nn.sigmoid(z)
        lf2 = jnp.log(lb + (1.0 - lb) * sig) * LOG2E
        kk[ci] = (1.0 - lb) * (1.0 - sig)
        qf[ci] = qq * jax.nn.sigmoid(qq)
        v16[ci] = i_ref[0, sl, :].astype(BF16)
        hi = lf2.astype(BF16)
        lo = (lf2 - hi.astype(F32)).astype(BF16)
        parts = jnp.concatenate([hi, lo], axis=0)
        dec[ci] = [jnp.exp2(jnp.dot(iv_ref[p * c:(p + 1) * c, :], parts,
                                    preferred_element_type=F32)) for p in range(nl + 2)]

    def stage2(ci):
        acc = [jnp.zeros((c, c), F32) for _ in range(nh)]
        for li in range(nl):
            qt = (qf[ci] * dec[ci][li]).astype(BF16)
            kt = (kk[ci] * dec[ci][li]).astype(BF16)
            mask = mk_ref[li * c:(li + 1) * c, :]
            for hd in range(nh):
                acc[hd] = acc[hd] + _dot_nt(head(qt, hd), head(kt, hd)) * mask
        a_tot[ci] = [a.astype(BF16) for a in acc]

    st = [st_ref[hd] for hd in range(nh)]

    def stage3(ci):
        sl = rows[ci]
        e_b = dec[ci][nl]
        e_r = dec[ci][nl + 1]
        qb = (qf[ci] * e_b).astype(BF16)
        kr = (kk[ci] * e_r).astype(BF16)
        qk = qf[ci] * kk[ci]
        v = i_ref[0, sl, :]
        outs = []
        for hd in range(nh):
            vh = head(v16[ci], hd)
            o = jnp.sum(head(qk, hd), axis=-1, keepdims=True) * head(v, hd)
            o = o + jnp.dot(a_tot[ci][hd], vh, preferred_element_type=F32)
            o = o + _dot_nt(head(qb, hd), st[hd].astype(BF16))
            st[hd] = st[hd] * head(e_b, hd)[c - 1:c, :] + lax.dot_general(
                vh, head(kr, hd), (((0,), (0,)), ((), ())), preferred_element_type=F32)
            outs.append(o * lax.rsqrt(jnp.mean(o * o, axis=-1, keepdims=True) + EPS))
        o = jnp.concatenate(outs, axis=1) * ng * jax.nn.sigmoid(g_ref[0, sl, :])
        o_ref[0, sl, :] = o.astype(o_ref.dtype)

    for step in range(nc + 2):
        if step < nc:
            stage1(step)
        if 0 <= step - 1 < nc:
            stage2(step - 1)
        if 0 <= step - 2 < nc:
            stage3(step - 2)
    for hd in range(nh):
        st_ref[hd] = st[hd]


def _hgrn(proj, lb_logits, norm_g, *, layer, hosted=(), host_scales=(), ts=1024, nh=2):
    b, s, _ = proj.shape
    kd = HG_DIM
    c = HG_CHUNK
    w = nh * kd
    ng = HG_HEADS // nh
    ns = s // ts
    nl = lb_logits.shape[0]
    intervals, masks = _hgrn_constants(c)
    h_in, h_out, h_shapes, h_bytes = _hosted_specs(
        hosted, b * ng * ns, lambda bi, h, i: (bi * ng + h) * ns + i)
    blocks = (2 * 4 * ts * w * 4 + 2 * ts * w * 2 + nh * kd * kd * 4 + 2 * intervals.size * 2
              + 2 * masks.size * 4 + 96 * c * w * 4 + h_bytes)

    def col(off):
        return lambda bi, h, i: (bi, i, off * ng + h)

    return pl.pallas_call(
        functools.partial(_hgrn_kernel, host_scales=tuple(host_scales), ts=ts, nh=nh,
                          layer=layer),
        grid=(b, ng, ns),
        in_specs=[
            pl.BlockSpec((nl, 1, 1, w), lambda bi, h, i: (0, h, 0, 0)),
            pl.BlockSpec((1, 1, w), lambda bi, h, i: (h, 0, 0)),
            pl.BlockSpec(intervals.shape, lambda bi, h, i: (0, 0)),
            pl.BlockSpec(masks.shape, lambda bi, h, i: (0, 0)),
            pl.BlockSpec((1, ts, w), col(0)),
            pl.BlockSpec((1, ts, w), col(1)),
            pl.BlockSpec((1, ts, w), col(2)),
            pl.BlockSpec((1, ts, w), col(3)),
        ] + h_in,
        out_specs=[pl.BlockSpec((1, ts, w), lambda bi, h, i: (bi, i, h))] + h_out,
        out_shape=[jax.ShapeDtypeStruct((b, s, HG_HEADS * kd), BF16)] + h_shapes,
        scratch_shapes=[pltpu.VMEM((nh, kd, kd), F32)],
        compiler_params=pltpu.CompilerParams(
            dimension_semantics=("parallel", "parallel", "arbitrary"),
            vmem_limit_bytes=_vmem_limit(blocks)),
        name="hgrn2",
    )(lb_logits.reshape(nl, ng, 1, w), norm_g.reshape(ng, 1, w), intervals, masks,
      proj, proj, proj, proj, *hosted)


def _merge_kernel(h_hbm, g_ref, y0_ref, y1_ref, y2_ref, wg0_ref, wg1_ref, wg2_ref,
                  wb0_ref, wb1_ref, wb2_ref, wout_ref, o_ref, h_buf, u_ref, mg_ref, h_sem, o_sem):
    i = pl.program_id(0)
    j = pl.program_id(1)
    nj = mg_ref.shape[0]
    tn = mg_ref.shape[2]
    tm = h_buf.shape[0]

    def h_copy(tile):
        return pltpu.make_async_copy(h_hbm.at[pl.ds(tile * tm, tm), :], h_buf, h_sem)

    def o_init():
        return pltpu.make_async_copy(h_buf, o_ref, o_sem)

    @pl.when((j == 0) & (i == 0))
    def _():
        h_copy(0).start()

    @pl.when(j == 0)
    def _():
        h_copy(i).wait()
        u_ref[...] = _rms(h_buf[...], g_ref[...]).astype(BF16)
        o_init().start()

    @pl.when(j == 1)
    def _():
        o_init().wait()

    @pl.when((j == 1) & (i + 1 < pl.num_programs(0)))
    def _():
        h_copy(i + 1).start()

    u = u_ref[...]
    merged = None
    for y_ref, wg_ref, wb_ref in ((y0_ref, wg0_ref, wb0_ref), (y1_ref, wg1_ref, wb1_ref),
                                  (y2_ref, wg2_ref, wb2_ref)):
        gate = jax.nn.sigmoid(jnp.dot(u, wg_ref[...], preferred_element_type=F32))
        term = gate * jnp.dot(y_ref[...], wb_ref[...], preferred_element_type=F32)
        merged = term if merged is None else merged + term
    mg_ref[j] = merged.astype(BF16)

    @pl.when(j == nj - 1)
    def _():
        acc = o_ref[...]
        for jj in range(nj):
            acc = acc + jnp.dot(mg_ref[jj], wout_ref[jj * tn:(jj + 1) * tn, :],
                                preferred_element_type=F32)
        o_ref[...] = acc


def _merge(h, gain, y0, y1, y2, w_in, gate_col0, wb0, wb1, wb2, wout, *, tm=512, tn=512):
    t, d = h.shape
    w = y0.shape[1]
    nj = d // tn
    assert nj >= 2
    blocks = (3 * tm * d * 4 + tm * d * 2 + tm * d * 2 + 2 * 3 * tm * w * 2
              + 2 * 3 * d * tn * 2 + 2 * 3 * w * tn * 2 + d * d * 2 + 8 * tm * tn * 4)

    def gate_spec(br):
        off = (gate_col0 + br * d) // tn
        return pl.BlockSpec((d, tn), lambda i, j: (0, off + j))

    y_spec = pl.BlockSpec((tm, w), lambda i, j: (i, 0))
    wb_spec = pl.BlockSpec((w, tn), lambda i, j: (0, j))
    return pl.pallas_call(
        _merge_kernel,
        grid=(t // tm, nj),
        in_specs=[
            pl.BlockSpec(memory_space=pl.ANY),
            pl.BlockSpec((1, d), lambda i, j: (0, 0)),
            y_spec, y_spec, y_spec,
            gate_spec(0), gate_spec(1), gate_spec(2),
            wb_spec, wb_spec, wb_spec,
            pl.BlockSpec((d, d), lambda i, j: (0, 0), pipeline_mode=pl.Buffered(1)),
        ],
        out_specs=pl.BlockSpec((tm, d), lambda i, j: (i, 0)),
        out_shape=jax.ShapeDtypeStruct((t, d), F32),
        scratch_shapes=[pltpu.VMEM((tm, d), F32), pltpu.VMEM((tm, d), BF16),
                        pltpu.VMEM((nj, tm, tn), BF16), pltpu.SemaphoreType.DMA(()),
                        pltpu.SemaphoreType.DMA(())],
        compiler_params=pltpu.CompilerParams(
            dimension_semantics=("arbitrary", "arbitrary"),
            vmem_limit_bytes=_vmem_limit(blocks)),
        name="merge_out",
    )(h, gain.reshape(1, d), y0, y1, y2, w_in, w_in, w_in, wb0, wb1, wb2, wout)


def kernel(x, mem, ffn1_norm, ffn1_w_gate, ffn1_w_up, ffn1_w_down, mix_norm, mem_norm, w_in,
           da_lambda_q1, da_lambda_k1, da_lambda_q2, da_lambda_k2, da_subln, hg_lb_logits, hg_norm,
           w_mem_kv, w_branch_da, w_branch_hg, w_branch_xa, w_out,
           ffn2_norm, ffn2_w_gate, ffn2_w_up, ffn2_w_down, final_norm):
    bsz, seq, d = x.shape
    depth = w_in.shape[0]
    t = bsz * seq
    da_w = DA_HEADS * 2 * DA_HEAD_DIM
    hg_w = HG_HEADS * HG_DIM
    xa_w = XA_HEADS * XA_HEAD_DIM
    slopes = jnp.asarray([2.0 ** (-8.0 * (i + 1) / DA_HEADS) for i in range(DA_HEADS)], F32)

    h = x.reshape(t, d)
    for l in range(depth):
        lambda_init = 0.8 - 0.6 * math.exp(-0.3 * l)
        h = _ffn(h, ffn1_norm[l], ffn1_w_gate[l].astype(BF16), ffn1_w_up[l].astype(BF16),
                 (0.5 * ffn1_w_down[l]).astype(BF16), final_norm, final_norm=False)

        w_in16 = w_in[l].astype(BF16)
        p_att, p_hg = _in_proj(h, mix_norm[l], w_in16, da_cols=3 * da_w, hg_cols=4 * hg_w,
                               xa_cols=xa_w)
        p_att = p_att.reshape(bsz, seq, -1)
        p_hg = p_hg.reshape(bsz, seq, -1)

        lam4 = jnp.stack([da_lambda_q1[l], da_lambda_k1[l], da_lambda_q2[l], da_lambda_k2[l]])
        y_da, wg2, wu2, wd2_half = _diff_attn(
            p_att, slopes, lam4, da_subln[l], lambda_init=lambda_init,
            hosted=(ffn2_w_gate[l], ffn2_w_up[l], ffn2_w_down[l]), host_scales=(1.0, 1.0, 0.5))
        y_hg, wb_da, wb_hg, wb_xa, wo = _hgrn(
            p_hg, hg_lb_logits, hg_norm[l], layer=l,
            hosted=(w_branch_da[l], w_branch_hg[l], w_branch_xa[l], w_out[l]),
            host_scales=(1.0, 1.0, 1.0, 1.0))
        kv = _norm_matmul(mem.reshape(-1, d), mem_norm[l], w_mem_kv[l].astype(BF16), BF16,
                          tm=512, name="mem_kv").reshape(bsz, mem.shape[1], -1)
        y_xa = _xattn(p_att, kv, q_col_block=3 * da_w // xa_w)

        h = _merge(h, mix_norm[l], y_da.reshape(t, da_w), y_hg.reshape(t, hg_w),
                   y_xa.reshape(t, xa_w), w_in16, 3 * da_w + 4 * hg_w + xa_w,
                   wb_da, wb_hg, wb_xa, wo)

        h = _ffn(h, ffn2_norm[l], wg2, wu2, wd2_half, final_norm, final_norm=(l == depth - 1))
    return h.reshape(bsz, seq, d)
```

```python
import functools
import math

import numpy as np
import jax
import jax.numpy as jnp
from jax import lax
from jax.experimental import pallas as pl
from jax.experimental.pallas import tpu as pltpu

F32 = jnp.float32
BF16 = jnp.bfloat16

EPS = 1e-6
DA_HEADS = 4
DA_HEAD_DIM = 128
HG_HEADS = 8
HG_DIM = 128
XA_HEADS = 4
XA_HEAD_DIM = 256
LOG2E = math.log2(math.e)
PROJ_TN = 1024

V7X_VMEM_BYTES = 64 * 1024 * 1024
V7X_LANES = 128

NEG_BIG = -1e30


def _vmem_limit(block_bytes):
    want = int(block_bytes * 1.25) + (4 << 20)
    return min(want, V7X_VMEM_BYTES - (4 << 20))


def _rms(x, g):
    return x * lax.rsqrt(jnp.mean(x * x, axis=-1, keepdims=True) + EPS) * g


COPY_ROWS = 64


def _copy_rows(src_ref, dst_ref):
    def rows(r, carry):
        sl = pl.ds(pl.multiple_of(r * COPY_ROWS, COPY_ROWS), COPY_ROWS)
        dst_ref[sl, :] = src_ref[sl, :]
        return carry

    lax.fori_loop(0, src_ref.shape[0] // COPY_ROWS, rows, 0)


def _dot_nt(a, b):
    return lax.dot_general(a, b, (((1,), (1,)), ((), ())), preferred_element_type=F32)


def _ffn_kernel(x_hbm, g_ref, wg_ref, wu_ref, wd_ref, fg_ref, *refs, host_scales, final_norm):
    nhost = len(host_scales)
    host_in, o_ref, host_out = refs[:nhost], refs[nhost], refs[nhost + 1:2 * nhost + 1]
    x_buf, xn_ref, x_sem = refs[2 * nhost + 1:]
    i = pl.program_id(0)
    j = pl.program_id(1)
    tm = x_buf.shape[0]

    def x_copy(tile):
        return pltpu.make_async_copy(x_hbm.at[pl.ds(tile * tm, tm), :], x_buf, x_sem)

    @pl.when((j == 0) & (i == 0))
    def _():
        x_copy(0).start()

    @pl.when(j == 0)
    def _():
        x_copy(i).wait()
        xn_ref[...] = _rms(x_buf[...], g_ref[...]).astype(BF16)
        _copy_rows(x_buf, o_ref)

    @pl.when((j == 1) & (i + 1 < pl.num_programs(0)))
    def _():
        x_copy(i + 1).start()

    _cast_hosted(host_in, host_out, host_scales)
    xn = xn_ref[...]
    a = jnp.dot(xn, wg_ref[...], preferred_element_type=F32)
    b = jnp.dot(xn, wu_ref[...], preferred_element_type=F32)
    hmid = (a * jax.nn.sigmoid(a) * b).astype(BF16)
    o_ref[...] += jnp.dot(hmid, wd_ref[...], preferred_element_type=F32)

    if final_norm:
        @pl.when(j == pl.num_programs(1) - 1)
        def _():
            o_ref[...] = _rms(o_ref[...], fg_ref[...])


FFN_HOST_STEPS = 8


def _ffn(x, gain, wg, wu, wd_half, final_gain, *, final_norm, hosted=(), host_scales=(),
         tm=1024, tf=512):
    wd = wd_half
    t, d = x.shape
    dff = wg.shape[1]
    nf = dff // tf
    assert nf >= max(2, FFN_HOST_STEPS)
    h_in, h_out, h_shapes, h_bytes = _hosted_specs(
        hosted, (t // tm) * FFN_HOST_STEPS,
        lambda i, j: i * FFN_HOST_STEPS + jnp.minimum(j, FFN_HOST_STEPS - 1))
    blocks = 3 * tm * d * 4 + tm * d * 2 + 2 * 3 * d * tf * 2 + 3 * tm * tf * 4 + h_bytes
    return pl.pallas_call(
        functools.partial(_ffn_kernel, host_scales=tuple(host_scales), final_norm=final_norm),
        grid=(t // tm, nf),
        in_specs=[
            pl.BlockSpec(memory_space=pl.ANY),
            pl.BlockSpec((1, d), lambda i, j: (0, 0)),
            pl.BlockSpec((d, tf), lambda i, j: (0, j)),
            pl.BlockSpec((d, tf), lambda i, j: (0, j)),
            pl.BlockSpec((tf, d), lambda i, j: (j, 0)),
            pl.BlockSpec((1, d), lambda i, j: (0, 0)),
        ] + h_in,
        out_specs=[pl.BlockSpec((tm, d), lambda i, j: (i, 0))] + h_out,
        out_shape=[jax.ShapeDtypeStruct((t, d), F32)] + h_shapes,
        scratch_shapes=[pltpu.VMEM((tm, d), F32), pltpu.VMEM((tm, d), BF16),
                        pltpu.SemaphoreType.DMA(())],
        compiler_params=pltpu.CompilerParams(
            dimension_semantics=("arbitrary", "arbitrary"),
            vmem_limit_bytes=_vmem_limit(blocks)),
        name="ffn_final" if final_norm else "ffn",
    )(x, gain.reshape(1, d), wg, wu, wd, final_gain.reshape(1, d), *hosted)


def _norm_matmul_kernel(x_ref, g_ref, w_ref, o_ref, xn_ref):
    @pl.when(pl.program_id(1) == 0)
    def _():
        xn_ref[...] = _rms(x_ref[...], g_ref[...]).astype(BF16)

    o_ref[...] = jnp.dot(xn_ref[...], w_ref[...], preferred_element_type=F32).astype(o_ref.dtype)


def _norm_matmul(x, gain, w, out_dtype, *, tm, tn=PROJ_TN, name):
    t, d = x.shape
    n_cols = w.shape[1]
    out_bytes = jnp.dtype(out_dtype).itemsize
    blocks = (2 * tm * d * 4 + tm * d * 2 + 2 * d * tn * 2 + 2 * tm * tn * out_bytes
              + tm * tn * 4)
    return pl.pallas_call(
        _norm_matmul_kernel,
        grid=(t // tm, n_cols // tn),
        in_specs=[
            pl.BlockSpec((tm, d), lambda i, j: (i, 0)),
            pl.BlockSpec((1, d), lambda i, j: (0, 0)),
            pl.BlockSpec((d, tn), lambda i, j: (0, j)),
        ],
        out_specs=pl.BlockSpec((tm, tn), lambda i, j: (i, j)),
        out_shape=jax.ShapeDtypeStruct((t, n_cols), out_dtype),
        scratch_shapes=[pltpu.VMEM((tm, d), BF16)],
        compiler_params=pltpu.CompilerParams(
            dimension_semantics=("parallel", "arbitrary"),
            vmem_limit_bytes=_vmem_limit(blocks)),
        name=name,
    )(x, gain.reshape(1, d), w)


def _in_proj_kernel(x_ref, g_ref, w_ref, att_ref, hg_ref, xn_ref, *, n_hg):
    j = pl.program_id(1)

    @pl.when(j == 0)
    def _():
        xn_ref[...] = _rms(x_ref[...], g_ref[...]).astype(BF16)

    res = jnp.dot(xn_ref[...], w_ref[...], preferred_element_type=F32)
    att_ref[...] = res.astype(att_ref.dtype)

    @pl.when(j < n_hg)
    def _():
        hg_ref[...] = res


def _in_proj(x, gain, w_in, *, da_cols, hg_cols, xa_cols, tm=1024, tn=PROJ_TN):
    t, d = x.shape
    da_b, hg_b, xa_b = da_cols // tn, hg_cols // tn, xa_cols // tn

    def w_col(j):
        a = j - hg_b
        return jnp.where(j < hg_b, da_b + j, jnp.where(a < da_b, a, a + hg_b))

    blocks = (2 * tm * d * 4 + tm * d * 2 + 2 * d * tn * 2 + 2 * tm * tn * 2 + 2 * tm * tn * 4
              + tm * tn * 4)
    return pl.pallas_call(
        functools.partial(_in_proj_kernel, n_hg=hg_b),
        grid=(t // tm, hg_b + da_b + xa_b),
        in_specs=[
            pl.BlockSpec((tm, d), lambda i, j: (i, 0)),
            pl.BlockSpec((1, d), lambda i, j: (0, 0)),
            pl.BlockSpec((d, tn), lambda i, j: (0, w_col(j))),
        ],
        out_specs=[
            pl.BlockSpec((tm, tn), lambda i, j: (i, jnp.maximum(j - hg_b, 0))),
            pl.BlockSpec((tm, tn), lambda i, j: (i, jnp.minimum(j, hg_b - 1))),
        ],
        out_shape=[jax.ShapeDtypeStruct((t, da_cols + xa_cols), BF16),
                   jax.ShapeDtypeStruct((t, hg_cols), F32)],
        scratch_shapes=[pltpu.VMEM((tm, d), BF16)],
        compiler_params=pltpu.CompilerParams(
            dimension_semantics=("parallel", "arbitrary"),
            vmem_limit_bytes=_vmem_limit(blocks)),
        name="in_proj",
    )(x, gain.reshape(1, d), w_in)


BF16_SUBLANES = 16


def _hosted_specs(weights, n_steps, step_index):
    in_specs, out_specs, out_shapes, nbytes = [], [], [], 0
    for w in weights:
        rows, cols = w.shape
        slab = rows // n_steps
        assert rows % n_steps == 0 and slab % BF16_SUBLANES == 0
        spec = pl.BlockSpec((slab, cols), lambda *g: (step_index(*g), 0))
        in_specs.append(spec)
        out_specs.append(spec)
        out_shapes.append(jax.ShapeDtypeStruct(w.shape, BF16))
        nbytes += 2 * slab * cols * (4 + 2)
    return in_specs, out_specs, out_shapes, nbytes


def _cast_hosted(src_refs, dst_refs, scales):
    for src, dst, scale in zip(src_refs, dst_refs, scales):
        x = src[...]
        dst[...] = (x if scale == 1.0 else x * scale).astype(BF16)


def _diff_attn_kernel(slope_ref, lam_ref, sg_ref, q_ref, k_ref, v_ref, *refs,
                      host_scales, tq, tr, tk, lambda_init):
    nhost = len(host_scales)
    host_in, o_ref, host_out = refs[:nhost], refs[nhost], refs[nhost + 1:2 * nhost + 1]
    m_ref, l_ref, acc_ref = refs[2 * nhost + 1:]
    h = pl.program_id(1)
    qi = pl.program_id(2)
    d = DA_HEAD_DIM
    hd = 2 * d
    nr = tq // tr
    scale2 = d ** -0.5 * LOG2E
    slope2 = slope_ref[h] * LOG2E

    m_ref[...] = jnp.full_like(m_ref, NEG_BIG)
    l_ref[...] = jnp.zeros_like(l_ref)
    acc_ref[...] = jnp.zeros_like(acc_ref)

    def lanes(x, width):
        return jnp.concatenate([x] * (width // V7X_LANES), axis=1)

    def tiles(row_tiles, k_start, width, masked_r):
        kb = k_ref[0, pl.ds(k_start, width), :]
        vb = v_ref[0, pl.ds(k_start, width), :]
        col = lax.broadcasted_iota(jnp.int32, (1, width), 1)
        chains = [(r, c) for r in row_tiles for c in range(2)]
        s_all = {}
        for r, c in chains:
            bias = slope2 * (col + (k_start - (qi * tq + r * tr))).astype(F32)
            q = q_ref[0, r * tr:(r + 1) * tr, c * d:(c + 1) * d]
            s = _dot_nt(q, kb[:, c * d:(c + 1) * d]) * scale2 + bias
            if r == masked_r:
                keep = (lax.broadcasted_iota(jnp.int32, (tr, width), 0)
                        >= lax.broadcasted_iota(jnp.int32, (tr, width), 1))
                s = jnp.where(keep, s, NEG_BIG)
            s_all[r, c] = s
        p_all, alpha_all = {}, {}
        for r, c in chains:
            idx = 2 * r + c
            s = s_all[r, c]
            m_old = m_ref[idx]
            m_new = jnp.maximum(m_old, jnp.max(s, axis=-1, keepdims=True))
            alpha = jnp.exp2(m_old - m_new)
            p = jnp.exp2(s - lanes(m_new, width))
            l_ref[idx] = alpha * l_ref[idx] + jnp.sum(p, axis=-1, keepdims=True)
            m_ref[idx] = m_new
            p_all[r, c] = p.astype(BF16)
            alpha_all[r, c] = alpha
        for r, c in chains:
            idx = 2 * r + c
            acc_ref[idx] = lanes(alpha_all[r, c], hd) * acc_ref[idx] + jnp.dot(
                p_all[r, c], vb, preferred_element_type=F32)

    def body(ki, carry):
        tiles(range(nr), pl.multiple_of(ki * tk, tk), tk, None)
        return carry

    lax.fori_loop(0, qi * (tq // tk), body, 0)
    _cast_hosted(host_in, host_out, host_scales)
    for cc in range(nr):
        tiles(range(cc, nr), pl.multiple_of(qi * tq + cc * tr, tr), tr, cc)

    lam4 = lam_ref[...]
    lam = (jnp.exp(jnp.sum(lam4[0:1] * lam4[1:2], axis=-1, keepdims=True))
           - jnp.exp(jnp.sum(lam4[2:3] * lam4[3:4], axis=-1, keepdims=True)) + lambda_init)
    for r in range(nr):
        o = (acc_ref[2 * r] / lanes(l_ref[2 * r], hd)
             - lam * (acc_ref[2 * r + 1] / lanes(l_ref[2 * r + 1], hd)))
        o = _rms(o, sg_ref[...]) * (1.0 - lambda_init)
        o_ref[0, r * tr:(r + 1) * tr, :] = o.astype(o_ref.dtype)


def _diff_attn(proj, slopes, lam4, subln_g, *, lambda_init, hosted=(), host_scales=(),
               tq=2048, tr=512, tk=512):
    b, s, _ = proj.shape
    hd = 2 * DA_HEAD_DIM
    nchain = 2 * (tq // tr)
    nq = s // tq
    h_in, h_out, h_shapes, h_bytes = _hosted_specs(
        hosted, b * DA_HEADS * nq, lambda bi, h, i: (bi * DA_HEADS + h) * nq + i)
    blocks = (2 * tq * hd * 2 + 2 * 2 * s * hd * 2 + 2 * tq * hd * 2
              + nchain * tr * (hd + 2 * V7X_LANES) * 4 + 4 * nchain * tr * tk * 4 + h_bytes)
    return pl.pallas_call(
        functools.partial(_diff_attn_kernel, host_scales=tuple(host_scales), tq=tq, tr=tr, tk=tk,
                          lambda_init=lambda_init),
        grid=(b, DA_HEADS, nq),
        in_specs=[
            pl.BlockSpec(memory_space=pltpu.SMEM),
            pl.BlockSpec((4, DA_HEAD_DIM), lambda bi, h, i: (0, 0)),
            pl.BlockSpec((1, hd), lambda bi, h, i: (0, 0)),
            pl.BlockSpec((1, tq, hd), lambda bi, h, i: (bi, i, h)),
            pl.BlockSpec((1, s, hd), lambda bi, h, i: (bi, 0, DA_HEADS + h)),
            pl.BlockSpec((1, s, hd), lambda bi, h, i: (bi, 0, 2 * DA_HEADS + h)),
        ] + h_in,
        out_specs=[pl.BlockSpec((1, tq, hd), lambda bi, h, i: (bi, i, h))] + h_out,
        out_shape=[jax.ShapeDtypeStruct((b, s, DA_HEADS * hd), BF16)] + h_shapes,
        scratch_shapes=[pltpu.VMEM((nchain, tr, V7X_LANES), F32),
                        pltpu.VMEM((nchain, tr, V7X_LANES), F32),
                        pltpu.VMEM((nchain, tr, hd), F32)],
        compiler_params=pltpu.CompilerParams(
            dimension_semantics=("parallel", "parallel", "arbitrary"),
            vmem_limit_bytes=_vmem_limit(blocks)),
        name="diff_attn",
    )(slopes, lam4, subln_g.reshape(1, hd), proj, proj, proj, *hosted)


def _xattn_kernel(q_ref, kv_ref, o_ref):
    hd = XA_HEAD_DIM
    for h in range(XA_HEADS):
        q = q_ref[0, :, h * hd:(h + 1) * hd]
        k = kv_ref[0, :, h * hd:(h + 1) * hd]
        v = kv_ref[0, :, (XA_HEADS + h) * hd:(XA_HEADS + h + 1) * hd]
        s = _dot_nt(q, k) * (hd ** -0.5)
        p = jnp.exp(s - jnp.max(s, axis=-1, keepdims=True))
        p = p / jnp.sum(p, axis=-1, keepdims=True)
        o_ref[0, :, h * hd:(h + 1) * hd] = jnp.dot(
            p.astype(BF16), v, preferred_element_type=F32).astype(o_ref.dtype)


def _xattn(proj, kv, *, q_col_block, tq=1024):
    b, s, _ = proj.shape
    m = kv.shape[1]
    w = XA_HEADS * XA_HEAD_DIM
    blocks = 2 * 2 * tq * w * 2 + 2 * m * 2 * w * 2 + 4 * 4 * tq * m * 4
    return pl.pallas_call(
        _xattn_kernel,
        grid=(b, s // tq),
        in_specs=[
            pl.BlockSpec((1, tq, w), lambda bi, i: (bi, i, q_col_block)),
            pl.BlockSpec((1, m, 2 * w), lambda bi, i: (bi, 0, 0)),
        ],
        out_specs=pl.BlockSpec((1, tq, w), lambda bi, i: (bi, i, 0)),
        out_shape=jax.ShapeDtypeStruct((b, s, w), BF16),
        compiler_params=pltpu.CompilerParams(
            dimension_semantics=("parallel", "parallel"),
            vmem_limit_bytes=_vmem_limit(blocks)),
        name="mem_xattn",
    )(proj, kv)


HG_CHUNK = 128


def _hgrn_levels(c):
    levels, m = [], 1
    while m < c:
        levels.append(m)
        m *= 2
    return levels


def _hgrn_constants(c):
    t = np.arange(c)[:, None]
    s = np.arange(c)[None, :]
    intervals, masks = [], []
    for m in _hgrn_levels(c):
        edge = (t // (2 * m)) * (2 * m) + m - 1
        upper = (t % (2 * m)) >= m
        intervals.append(np.where(upper, (s > edge) & (s <= t), (s > t) & (s <= edge)))
        masks.append(upper & ((s // (2 * m)) == (t // (2 * m))) & ((s % (2 * m)) < m))
    intervals.append(s <= t)
    intervals.append(s > t)
    iv = np.concatenate(intervals, 0)
    return (jnp.asarray(np.concatenate([iv, iv], 1), BF16),
            jnp.asarray(np.concatenate(masks, 0), F32))


def _hgrn_kernel(lbl_ref, ng_ref, iv_ref, mk_ref, q_ref, f_ref, i_ref, g_ref, *refs,
                 host_scales, ts, nh, layer):
    nhost = len(host_scales)
    host_in, o_ref, host_out = refs[:nhost], refs[nhost], refs[nhost + 1:2 * nhost + 1]
    st_ref = refs[2 * nhost + 1]
    c = HG_CHUNK
    kd = HG_DIM
    levels = _hgrn_levels(c)
    nl = len(levels)

    @pl.when(pl.program_id(2) == 0)
    def _():
        st_ref[...] = jnp.zeros_like(st_ref)

    _cast_hosted(host_in, host_out, host_scales)

    logits = [lbl_ref[r, 0] for r in range(lbl_ref.shape[0])]
    mx = functools.reduce(jnp.maximum, logits)
    ex = [jnp.exp(v - mx) for v in logits]
    lb = sum(ex[1:layer + 1], ex[0]) / sum(ex[1:], ex[0])
    ng = ng_ref[0]

    def head(x, hd):
        return x[:, hd * kd:(hd + 1) * kd]

    nc = ts // c
    rows = [slice(ci * c, (ci + 1) * c) for ci in range(nc)]

    qf, kk, v16, dec, a_tot = {}, {}, {}, {}, {}

    def stage1(ci):
        sl = rows[ci]
        z = f_ref[0, sl, :]
        qq = q_ref[0, sl, :]
        sig = jax.nn.sigmoid(z)
        lf2 = jnp.log(lb + (1.0 - lb) * sig) * LOG2E
        kk[ci] = (1.0 - lb) * (1.0 - sig)
        qf[ci] = qq * jax.nn.sigmoid(qq)
        v16[ci] = i_ref[0, sl, :].astype(BF16)
        hi = lf2.astype(BF16)
        lo = (lf2 - hi.astype(F32)).astype(BF16)
        parts = jnp.concatenate([hi, lo], axis=0)
        dec[ci] = [jnp.exp2(jnp.dot(iv_ref[p * c:(p + 1) * c, :], parts,
                                    preferred_element_type=F32)) for p in range(nl + 2)]

    def stage2(ci):
        acc = [jnp.zeros((c, c), F32) for _ in range(nh)]
        for li in range(nl):
            qt = (qf[ci] * dec[ci][li]).astype(BF16)
            kt = (kk[ci] * dec[ci][li]).astype(BF16)
            mask = mk_ref[li * c:(li + 1) * c, :]
            for hd in range(nh):
                acc[hd] = acc[hd] + _dot_nt(head(qt, hd), head(kt, hd)) * mask
        a_tot[ci] = [a.astype(BF16) for a in acc]

    st = [st_ref[hd] for hd in range(nh)]

    def stage3(ci):
        sl = rows[ci]
        e_b = dec[ci][nl]
        e_r = dec[ci][nl + 1]
        qb = (qf[ci] * e_b).astype(BF16)
        kr = (kk[ci] * e_r).astype(BF16)
        qk = qf[ci] * kk[ci]
        v = i_ref[0, sl, :]
        outs = []
        for hd in range(nh):
            vh = head(v16[ci], hd)
            o = jnp.sum(head(qk, hd), axis=-1, keepdims=True) * head(v, hd)
            o = o + jnp.dot(a_tot[ci][hd], vh, preferred_element_type=F32)
            o = o + _dot_nt(head(qb, hd), st[hd].astype(BF16))
            st[hd] = st[hd] * head(e_b, hd)[c - 1:c, :] + lax.dot_general(
                vh, head(kr, hd), (((0,), (0,)), ((), ())), preferred_element_type=F32)
            outs.append(o * lax.rsqrt(jnp.mean(o * o, axis=-1, keepdims=True) + EPS))
        o = jnp.concatenate(outs, axis=1) * ng * jax.nn.sigmoid(g_ref[0, sl, :])
        o_ref[0, sl, :] = o.astype(o_ref.dtype)

    for step in range(nc + 2):
        if step < nc:
            stage1(step)
        if 0 <= step - 1 < nc:
            stage2(step - 1)
        if 0 <= step - 2 < nc:
            stage3(step - 2)
    for hd in range(nh):
        st_ref[hd] = st[hd]


def _hgrn(proj, lb_logits, norm_g, *, layer, hosted=(), host_scales=(), ts=1024, nh=2):
    b, s, _ = proj.shape
    kd = HG_DIM
    c = HG_CHUNK
    w = nh * kd
    ng = HG_HEADS // nh
    ns = s // ts
    nl = lb_logits.shape[0]
    intervals, masks = _hgrn_constants(c)
    h_in, h_out, h_shapes, h_bytes = _hosted_specs(
        hosted, b * ng * ns, lambda bi, h, i: (bi * ng + h) * ns + i)
    blocks = (2 * 4 * ts * w * 4 + 2 * ts * w * 2 + nh * kd * kd * 4 + 2 * intervals.size * 2
              + 2 * masks.size * 4 + 96 * c * w * 4 + h_bytes)

    def col(off):
        return lambda bi, h, i: (bi, i, off * ng + h)

    return pl.pallas_call(
        functools.partial(_hgrn_kernel, host_scales=tuple(host_scales), ts=ts, nh=nh,
                          layer=layer),
        grid=(b, ng, ns),
        in_specs=[
            pl.BlockSpec((nl, 1, 1, w), lambda bi, h, i: (0, h, 0, 0)),
            pl.BlockSpec((1, 1, w), lambda bi, h, i: (h, 0, 0)),
            pl.BlockSpec(intervals.shape, lambda bi, h, i: (0, 0)),
            pl.BlockSpec(masks.shape, lambda bi, h, i: (0, 0)),
            pl.BlockSpec((1, ts, w), col(0)),
            pl.BlockSpec((1, ts, w), col(1)),
            pl.BlockSpec((1, ts, w), col(2)),
            pl.BlockSpec((1, ts, w), col(3)),
        ] + h_in,
        out_specs=[pl.BlockSpec((1, ts, w), lambda bi, h, i: (bi, i, h))] + h_out,
        out_shape=[jax.ShapeDtypeStruct((b, s, HG_HEADS * kd), BF16)] + h_shapes,
        scratch_shapes=[pltpu.VMEM((nh, kd, kd), F32)],
        compiler_params=pltpu.CompilerParams(
            dimension_semantics=("parallel", "parallel", "arbitrary"),
            vmem_limit_bytes=_vmem_limit(blocks)),
        name="hgrn2",
    )(lb_logits.reshape(nl, ng, 1, w), norm_g.reshape(ng, 1, w), intervals, masks,
      proj, proj, proj, proj, *hosted)


def _merge_kernel(h_hbm, g_ref, y0_ref, y1_ref, y2_ref, wg0_ref, wg1_ref, wg2_ref,
                  wb0_ref, wb1_ref, wb2_ref, wout_ref, o_ref, h_buf, u_ref, mg_ref, h_sem, o_sem):
    i = pl.program_id(0)
    j = pl.program_id(1)
    nj = mg_ref.shape[0]
    tn = mg_ref.shape[2]
    tm = h_buf.shape[0]

    def h_copy(tile):
        return pltpu.make_async_copy(h_hbm.at[pl.ds(tile * tm, tm), :], h_buf, h_sem)

    def o_init():
        return pltpu.make_async_copy(h_buf, o_ref, o_sem)

    @pl.when((j == 0) & (i == 0))
    def _():
        h_copy(0).start()

    @pl.when(j == 0)
    def _():
        h_copy(i).wait()
        u_ref[...] = _rms(h_buf[...], g_ref[...]).astype(BF16)
        o_init().start()

    @pl.when(j == 1)
    def _():
        o_init().wait()

    @pl.when((j == 1) & (i + 1 < pl.num_programs(0)))
    def _():
        h_copy(i + 1).start()

    u = u_ref[...]
    merged = None
    for y_ref, wg_ref, wb_ref in ((y0_ref, wg0_ref, wb0_ref), (y1_ref, wg1_ref, wb1_ref),
                                  (y2_ref, wg2_ref, wb2_ref)):
        gate = jax.nn.sigmoid(jnp.dot(u, wg_ref[...], preferred_element_type=F32))
        term = gate * jnp.dot(y_ref[...], wb_ref[...], preferred_element_type=F32)
        merged = term if merged is None else merged + term
    mg_ref[j] = merged.astype(BF16)

    @pl.when(j == nj - 1)
    def _():
        acc = o_ref[...]
        for jj in range(nj):
            acc = acc + jnp.dot(mg_ref[jj], wout_ref[jj * tn:(jj + 1) * tn, :],
                                preferred_element_type=F32)
        o_ref[...] = acc


def _merge(h, gain, y0, y1, y2, w_in, gate_col0, wb0, wb1, wb2, wout, *, tm=512, tn=512):
    t, d = h.shape
    w = y0.shape[1]
    nj = d // tn
    assert nj >= 2
    blocks = (3 * tm * d * 4 + tm * d * 2 + tm * d * 2 + 2 * 3 * tm * w * 2
              + 2 * 3 * d * tn * 2 + 2 * 3 * w * tn * 2 + d * d * 2 + 8 * tm * tn * 4)

    def gate_spec(br):
        off = (gate_col0 + br * d) // tn
        return pl.BlockSpec((d, tn), lambda i, j: (0, off + j))

    y_spec = pl.BlockSpec((tm, w), lambda i, j: (i, 0))
    wb_spec = pl.BlockSpec((w, tn), lambda i, j: (0, j))
    return pl.pallas_call(
        _merge_kernel,
        grid=(t // tm, nj),
        in_specs=[
            pl.BlockSpec(memory_space=pl.ANY),
            pl.BlockSpec((1, d), lambda i, j: (0, 0)),
            y_spec, y_spec, y_spec,
            gate_spec(0), gate_spec(1), gate_spec(2),
            wb_spec, wb_spec, wb_spec,
            pl.BlockSpec((d, d), lambda i, j: (0, 0), pipeline_mode=pl.Buffered(1)),
        ],
        out_specs=pl.BlockSpec((tm, d), lambda i, j: (i, 0)),
        out_shape=jax.ShapeDtypeStruct((t, d), F32),
        scratch_shapes=[pltpu.VMEM((tm, d), F32), pltpu.VMEM((tm, d), BF16),
                        pltpu.VMEM((nj, tm, tn), BF16), pltpu.SemaphoreType.DMA(()),
                        pltpu.SemaphoreType.DMA(())],
        compiler_params=pltpu.CompilerParams(
            dimension_semantics=("arbitrary", "arbitrary"),
            vmem_limit_bytes=_vmem_limit(blocks)),
        name="merge_out",
    )(h, gain.reshape(1, d), y0, y1, y2, w_in, w_in, w_in, wb0, wb1, wb2, wout)


def kernel(x, mem, ffn1_norm, ffn1_w_gate, ffn1_w_up, ffn1_w_down, mix_norm, mem_norm, w_in,
           da_lambda_q1, da_lambda_k1, da_lambda_q2, da_lambda_k2, da_subln, hg_lb_logits, hg_norm,
           w_mem_kv, w_branch_da, w_branch_hg, w_branch_xa, w_out,
           ffn2_norm, ffn2_w_gate, ffn2_w_up, ffn2_w_down, final_norm):
    bsz, seq, d = x.shape
    depth = w_in.shape[0]
    t = bsz * seq
    da_w = DA_HEADS * 2 * DA_HEAD_DIM
    hg_w = HG_HEADS * HG_DIM
    xa_w = XA_HEADS * XA_HEAD_DIM
    slopes = jnp.asarray([2.0 ** (-8.0 * (i + 1) / DA_HEADS) for i in range(DA_HEADS)], F32)

    h = x.reshape(t, d)
    for l in range(depth):
        lambda_init = 0.8 - 0.6 * math.exp(-0.3 * l)
        h, w_in16 = _ffn(h, ffn1_norm[l], ffn1_w_gate[l].astype(BF16), ffn1_w_up[l].astype(BF16),
                         (0.5 * ffn1_w_down[l]).astype(BF16), final_norm, final_norm=False,
                         hosted=(w_in[l],), host_scales=(1.0,))
        p_att, p_hg = _in_proj(h, mix_norm[l], w_in16, da_cols=3 * da_w, hg_cols=4 * hg_w,
                               xa_cols=xa_w)
        p_att = p_att.reshape(bsz, seq, -1)
        p_hg = p_hg.reshape(bsz, seq, -1)

        lam4 = jnp.stack([da_lambda_q1[l], da_lambda_k1[l], da_lambda_q2[l], da_lambda_k2[l]])
        y_da, wg2, wu2, wd2_half = _diff_attn(
            p_att, slopes, lam4, da_subln[l], lambda_init=lambda_init,
            hosted=(ffn2_w_gate[l], ffn2_w_up[l], ffn2_w_down[l]), host_scales=(1.0, 1.0, 0.5))
        y_hg, wb_da, wb_hg, wb_xa, wo = _hgrn(
            p_hg, hg_lb_logits, hg_norm[l], layer=l,
            hosted=(w_branch_da[l], w_branch_hg[l], w_branch_xa[l], w_out[l]),
            host_scales=(1.0, 1.0, 1.0, 1.0))
        kv = _norm_matmul(mem.reshape(-1, d), mem_norm[l], w_mem_kv[l].astype(BF16), BF16,
                          tm=512, name="mem_kv").reshape(bsz, mem.shape[1], -1)
        y_xa = _xattn(p_att, kv, q_col_block=3 * da_w // xa_w)

        h = _merge(h, mix_norm[l], y_da.reshape(t, da_w), y_hg.reshape(t, hg_w),
                   y_xa.reshape(t, xa_w), w_in16, 3 * da_w + 4 * hg_w + xa_w,
                   wb_da, wb_hg, wb_xa, wo)

        h, = _ffn(h, ffn2_norm[l], wg2, wu2, wd2_half, final_norm, final_norm=(l == depth - 1))
    return h.reshape(bsz, seq, d)
```

```python
import functools
import math

import numpy as np
import jax
import jax.numpy as jnp
from jax import lax
from jax.experimental import pallas as pl
from jax.experimental.pallas import tpu as pltpu

F32 = jnp.float32
BF16 = jnp.bfloat16

EPS = 1e-6
DA_HEADS = 4
DA_HEAD_DIM = 128
HG_HEADS = 8
HG_DIM = 128
XA_HEADS = 4
XA_HEAD_DIM = 256
LOG2E = math.log2(math.e)
PROJ_TN = 1024

V7X_VMEM_BYTES = 64 * 1024 * 1024
V7X_LANES = 128

NEG_BIG = -1e30


def _vmem_limit(block_bytes):
    want = int(block_bytes * 1.25) + (4 << 20)
    return min(want, V7X_VMEM_BYTES - (4 << 20))


def _rms(x, g):
    return x * lax.rsqrt(jnp.mean(x * x, axis=-1, keepdims=True) + EPS) * g


COPY_ROWS = 64


def _copy_rows(src_ref, dst_ref):
    def rows(r, carry):
        sl = pl.ds(pl.multiple_of(r * COPY_ROWS, COPY_ROWS), COPY_ROWS)
        dst_ref[sl, :] = src_ref[sl, :]
        return carry

    lax.fori_loop(0, src_ref.shape[0] // COPY_ROWS, rows, 0)


def _dot_nt(a, b):
    return lax.dot_general(a, b, (((1,), (1,)), ((), ())), preferred_element_type=F32)


def _ffn_kernel(x_hbm, g_ref, wg_ref, wu_ref, wd_ref, fg_ref, *refs, host_scales, wd_scale,
                final_norm):
    nhost = len(host_scales)
    host_in, o_ref, host_out = refs[:nhost], refs[nhost], refs[nhost + 1:2 * nhost + 1]
    x_buf, xn_ref, x_sem = refs[2 * nhost + 1:]
    i = pl.program_id(0)
    j = pl.program_id(1)
    tm = x_buf.shape[0]

    def x_copy(tile):
        return pltpu.make_async_copy(x_hbm.at[pl.ds(tile * tm, tm), :], x_buf, x_sem)

    @pl.when((j == 0) & (i == 0))
    def _():
        x_copy(0).start()

    @pl.when(j == 0)
    def _():
        x_copy(i).wait()
        xn_ref[...] = _rms(x_buf[...], g_ref[...]).astype(BF16)
        _copy_rows(x_buf, o_ref)

    @pl.when((j == 1) & (i + 1 < pl.num_programs(0)))
    def _():
        x_copy(i + 1).start()

    _cast_hosted(host_in, host_out, host_scales)
    def bf16(ref, scale=1.0):
        w = ref[...]
        return w if w.dtype == BF16 else (w if scale == 1.0 else w * scale).astype(BF16)

    xn = xn_ref[...]
    a = jnp.dot(xn, bf16(wg_ref), preferred_element_type=F32)
    b = jnp.dot(xn, bf16(wu_ref), preferred_element_type=F32)
    hmid = (a * jax.nn.sigmoid(a) * b).astype(BF16)
    o_ref[...] += jnp.dot(hmid, bf16(wd_ref, wd_scale), preferred_element_type=F32)

    if final_norm:
        @pl.when(j == pl.num_programs(1) - 1)
        def _():
            o_ref[...] = _rms(o_ref[...], fg_ref[...])


FFN_HOST_STEPS = 8


def _ffn(x, gain, wg, wu, wd, final_gain, *, wd_scale, final_norm, hosted=(), host_scales=(),
         tm=1024, tf=512):
    t, d = x.shape
    dff = wg.shape[1]
    nf = dff // tf
    assert nf >= max(2, FFN_HOST_STEPS)
    h_in, h_out, h_shapes, h_bytes = _hosted_specs(
        hosted, (t // tm) * FFN_HOST_STEPS,
        lambda i, j: i * FFN_HOST_STEPS + jnp.minimum(j, FFN_HOST_STEPS - 1))
    w_bytes = jnp.dtype(wg.dtype).itemsize
    blocks = (3 * tm * d * 4 + tm * d * 2 + 2 * 3 * d * tf * w_bytes + 3 * tm * tf * 4
              + h_bytes)
    return pl.pallas_call(
        functools.partial(_ffn_kernel, host_scales=tuple(host_scales), wd_scale=wd_scale,
                          final_norm=final_norm),
        grid=(t // tm, nf),
        in_specs=[
            pl.BlockSpec(memory_space=pl.ANY),
            pl.BlockSpec((1, d), lambda i, j: (0, 0)),
            pl.BlockSpec((d, tf), lambda i, j: (0, j)),
            pl.BlockSpec((d, tf), lambda i, j: (0, j)),
            pl.BlockSpec((tf, d), lambda i, j: (j, 0)),
            pl.BlockSpec((1, d), lambda i, j: (0, 0)),
        ] + h_in,
        out_specs=[pl.BlockSpec((tm, d), lambda i, j: (i, 0))] + h_out,
        out_shape=[jax.ShapeDtypeStruct((t, d), F32)] + h_shapes,
        scratch_shapes=[pltpu.VMEM((tm, d), F32), pltpu.VMEM((tm, d), BF16),
                        pltpu.SemaphoreType.DMA(())],
        compiler_params=pltpu.CompilerParams(
            dimension_semantics=("arbitrary", "arbitrary"),
            vmem_limit_bytes=_vmem_limit(blocks)),
        name="ffn_final" if final_norm else "ffn",
    )(x, gain.reshape(1, d), wg, wu, wd, final_gain.reshape(1, d), *hosted)


def _norm_matmul_kernel(x_ref, g_ref, w_ref, o_ref, xn_ref):
    @pl.when(pl.program_id(1) == 0)
    def _():
        xn_ref[...] = _rms(x_ref[...], g_ref[...]).astype(BF16)

    o_ref[...] = jnp.dot(xn_ref[...], w_ref[...], preferred_element_type=F32).astype(o_ref.dtype)


def _norm_matmul(x, gain, w, out_dtype, *, tm, tn=PROJ_TN, name):
    t, d = x.shape
    n_cols = w.shape[1]
    out_bytes = jnp.dtype(out_dtype).itemsize
    blocks = (2 * tm * d * 4 + tm * d * 2 + 2 * d * tn * 2 + 2 * tm * tn * out_bytes
              + tm * tn * 4)
    return pl.pallas_call(
        _norm_matmul_kernel,
        grid=(t // tm, n_cols // tn),
        in_specs=[
            pl.BlockSpec((tm, d), lambda i, j: (i, 0)),
            pl.BlockSpec((1, d), lambda i, j: (0, 0)),
            pl.BlockSpec((d, tn), lambda i, j: (0, j)),
        ],
        out_specs=pl.BlockSpec((tm, tn), lambda i, j: (i, j)),
        out_shape=jax.ShapeDtypeStruct((t, n_cols), out_dtype),
        scratch_shapes=[pltpu.VMEM((tm, d), BF16)],
        compiler_params=pltpu.CompilerParams(
            dimension_semantics=("parallel", "arbitrary"),
            vmem_limit_bytes=_vmem_limit(blocks)),
        name=name,
    )(x, gain.reshape(1, d), w)


def _in_proj_kernel(x_ref, g_ref, w_ref, att_ref, hg_ref, xn_ref, *, n_hg):
    j = pl.program_id(1)

    @pl.when(j == 0)
    def _():
        xn_ref[...] = _rms(x_ref[...], g_ref[...]).astype(BF16)

    res = jnp.dot(xn_ref[...], w_ref[...], preferred_element_type=F32)
    att_ref[...] = res.astype(att_ref.dtype)

    @pl.when(j < n_hg)
    def _():
        hg_ref[...] = res


def _in_proj(x, gain, w_in, *, da_cols, hg_cols, xa_cols, tm=1024, tn=PROJ_TN):
    t, d = x.shape
    da_b, hg_b, xa_b = da_cols // tn, hg_cols // tn, xa_cols // tn

    def w_col(j):
        a = j - hg_b
        return jnp.where(j < hg_b, da_b + j, jnp.where(a < da_b, a, a + hg_b))

    blocks = (2 * tm * d * 4 + tm * d * 2 + 2 * d * tn * 2 + 2 * tm * tn * 2 + 2 * tm * tn * 4
              + tm * tn * 4)
    return pl.pallas_call(
        functools.partial(_in_proj_kernel, n_hg=hg_b),
        grid=(t // tm, hg_b + da_b + xa_b),
        in_specs=[
            pl.BlockSpec((tm, d), lambda i, j: (i, 0)),
            pl.BlockSpec((1, d), lambda i, j: (0, 0)),
            pl.BlockSpec((d, tn), lambda i, j: (0, w_col(j))),
        ],
        out_specs=[
            pl.BlockSpec((tm, tn), lambda i, j: (i, jnp.maximum(j - hg_b, 0))),
            pl.BlockSpec((tm, tn), lambda i, j: (i, jnp.minimum(j, hg_b - 1))),
        ],
        out_shape=[jax.ShapeDtypeStruct((t, da_cols + xa_cols), BF16),
                   jax.ShapeDtypeStruct((t, hg_cols), F32)],
        scratch_shapes=[pltpu.VMEM((tm, d), BF16)],
        compiler_params=pltpu.CompilerParams(
            dimension_semantics=("parallel", "arbitrary"),
            vmem_limit_bytes=_vmem_limit(blocks)),
        name="in_proj",
    )(x, gain.reshape(1, d), w_in)


BF16_SUBLANES = 16


def _hosted_specs(weights, n_steps, step_index):
    in_specs, out_specs, out_shapes, nbytes = [], [], [], 0
    for w in weights:
        rows, cols = w.shape
        slab = rows // n_steps
        assert rows % n_steps == 0 and slab % BF16_SUBLANES == 0
        spec = pl.BlockSpec((slab, cols), lambda *g: (step_index(*g), 0))
        in_specs.append(spec)
        out_specs.append(spec)
        out_shapes.append(jax.ShapeDtypeStruct(w.shape, BF16))
        nbytes += 2 * slab * cols * (4 + 2)
    return in_specs, out_specs, out_shapes, nbytes


def _cast_hosted(src_refs, dst_refs, scales):
    for src, dst, scale in zip(src_refs, dst_refs, scales):
        x = src[...]
        dst[...] = (x if scale == 1.0 else x * scale).astype(BF16)


def _diff_attn_kernel(slope_ref, lam_ref, sg_ref, q_ref, k_ref, v_ref, *refs,
                      host_scales, tq, tr, tk, lambda_init):
    nhost = len(host_scales)
    host_in, o_ref, host_out = refs[:nhost], refs[nhost], refs[nhost + 1:2 * nhost + 1]
    m_ref, l_ref, acc_ref = refs[2 * nhost + 1:]
    h = pl.program_id(1)
    qi = pl.program_id(2)
    d = DA_HEAD_DIM
    hd = 2 * d
    nr = tq // tr
    scale2 = d ** -0.5 * LOG2E
    slope2 = slope_ref[h] * LOG2E

    m_ref[...] = jnp.full_like(m_ref, NEG_BIG)
    l_ref[...] = jnp.zeros_like(l_ref)
    acc_ref[...] = jnp.zeros_like(acc_ref)

    def lanes(x, width):
        return jnp.concatenate([x] * (width // V7X_LANES), axis=1)

    def tiles(row_tiles, k_start, width, masked_r):
        kb = k_ref[0, pl.ds(k_start, width), :]
        vb = v_ref[0, pl.ds(k_start, width), :]
        col = lax.broadcasted_iota(jnp.int32, (1, width), 1)
        chains = [(r, c) for r in row_tiles for c in range(2)]
        s_all = {}
        for r, c in chains:
            bias = slope2 * (col + (k_start - (qi * tq + r * tr))).astype(F32)
            q = q_ref[0, r * tr:(r + 1) * tr, c * d:(c + 1) * d]
            s = _dot_nt(q, kb[:, c * d:(c + 1) * d]) * scale2 + bias
            if r == masked_r:
                keep = (lax.broadcasted_iota(jnp.int32, (tr, width), 0)
                        >= lax.broadcasted_iota(jnp.int32, (tr, width), 1))
                s = jnp.where(keep, s, NEG_BIG)
            s_all[r, c] = s
        p_all, alpha_all = {}, {}
        for r, c in chains:
            idx = 2 * r + c
            s = s_all[r, c]
            m_old = m_ref[idx]
            m_new = jnp.maximum(m_old, jnp.max(s, axis=-1, keepdims=True))
            alpha = jnp.exp2(m_old - m_new)
            p = jnp.exp2(s - lanes(m_new, width))
            l_ref[idx] = alpha * l_ref[idx] + jnp.sum(p, axis=-1, keepdims=True)
            m_ref[idx] = m_new
            p_all[r, c] = p.astype(BF16)
            alpha_all[r, c] = alpha
        for r, c in chains:
            idx = 2 * r + c
            acc_ref[idx] = lanes(alpha_all[r, c], hd) * acc_ref[idx] + jnp.dot(
                p_all[r, c], vb, preferred_element_type=F32)

    def body(ki, carry):
        tiles(range(nr), pl.multiple_of(ki * tk, tk), tk, None)
        return carry

    lax.fori_loop(0, qi * (tq // tk), body, 0)
    _cast_hosted(host_in, host_out, host_scales)
    for cc in range(nr):
        tiles(range(cc, nr), pl.multiple_of(qi * tq + cc * tr, tr), tr, cc)

    lam4 = lam_ref[...]
    lam = (jnp.exp(jnp.sum(lam4[0:1] * lam4[1:2], axis=-1, keepdims=True))
           - jnp.exp(jnp.sum(lam4[2:3] * lam4[3:4], axis=-1, keepdims=True)) + lambda_init)
    for r in range(nr):
        o = (acc_ref[2 * r] / lanes(l_ref[2 * r], hd)
             - lam * (acc_ref[2 * r + 1] / lanes(l_ref[2 * r + 1], hd)))
        o = _rms(o, sg_ref[...]) * (1.0 - lambda_init)
        o_ref[0, r * tr:(r + 1) * tr, :] = o.astype(o_ref.dtype)


def _diff_attn(proj, slopes, lam4, subln_g, *, lambda_init, hosted=(), host_scales=(),
               tq=2048, tr=512, tk=512):
    b, s, _ = proj.shape
    hd = 2 * DA_HEAD_DIM
    nchain = 2 * (tq // tr)
    nq = s // tq
    h_in, h_out, h_shapes, h_bytes = _hosted_specs(
        hosted, b * DA_HEADS * nq, lambda bi, h, i: (bi * DA_HEADS + h) * nq + i)
    blocks = (2 * tq * hd * 2 + 2 * 2 * s * hd * 2 + 2 * tq * hd * 2
              + nchain * tr * (hd + 2 * V7X_LANES) * 4 + 4 * nchain * tr * tk * 4 + h_bytes)
    return pl.pallas_call(
        functools.partial(_diff_attn_kernel, host_scales=tuple(host_scales), tq=tq, tr=tr, tk=tk,
                          lambda_init=lambda_init),
        grid=(b, DA_HEADS, nq),
        in_specs=[
            pl.BlockSpec(memory_space=pltpu.SMEM),
            pl.BlockSpec((4, DA_HEAD_DIM), lambda bi, h, i: (0, 0)),
            pl.BlockSpec((1, hd), lambda bi, h, i: (0, 0)),
            pl.BlockSpec((1, tq, hd), lambda bi, h, i: (bi, i, h)),
            pl.BlockSpec((1, s, hd), lambda bi, h, i: (bi, 0, DA_HEADS + h)),
            pl.BlockSpec((1, s, hd), lambda bi, h, i: (bi, 0, 2 * DA_HEADS + h)),
        ] + h_in,
        out_specs=[pl.BlockSpec((1, tq, hd), lambda bi, h, i: (bi, i, h))] + h_out,
        out_shape=[jax.ShapeDtypeStruct((b, s, DA_HEADS * hd), BF16)] + h_shapes,
        scratch_shapes=[pltpu.VMEM((nchain, tr, V7X_LANES), F32),
                        pltpu.VMEM((nchain, tr, V7X_LANES), F32),
                        pltpu.VMEM((nchain, tr, hd), F32)],
        compiler_params=pltpu.CompilerParams(
            dimension_semantics=("parallel", "parallel", "arbitrary"),
            vmem_limit_bytes=_vmem_limit(blocks)),
        name="diff_attn",
    )(slopes, lam4, subln_g.reshape(1, hd), proj, proj, proj, *hosted)


def _xattn_kernel(q_ref, kv_ref, o_ref):
    hd = XA_HEAD_DIM
    for h in range(XA_HEADS):
        q = q_ref[0, :, h * hd:(h + 1) * hd]
        k = kv_ref[0, :, h * hd:(h + 1) * hd]
        v = kv_ref[0, :, (XA_HEADS + h) * hd:(XA_HEADS + h + 1) * hd]
        s = _dot_nt(q, k) * (hd ** -0.5)
        p = jnp.exp(s - jnp.max(s, axis=-1, keepdims=True))
        p = p / jnp.sum(p, axis=-1, keepdims=True)
        o_ref[0, :, h * hd:(h + 1) * hd] = jnp.dot(
            p.astype(BF16), v, preferred_element_type=F32).astype(o_ref.dtype)


def _xattn(proj, kv, *, q_col_block, tq=1024):
    b, s, _ = proj.shape
    m = kv.shape[1]
    w = XA_HEADS * XA_HEAD_DIM
    blocks = 2 * 2 * tq * w * 2 + 2 * m * 2 * w * 2 + 4 * 4 * tq * m * 4
    return pl.pallas_call(
        _xattn_kernel,
        grid=(b, s // tq),
        in_specs=[
            pl.BlockSpec((1, tq, w), lambda bi, i: (bi, i, q_col_block)),
            pl.BlockSpec((1, m, 2 * w), lambda bi, i: (bi, 0, 0)),
        ],
        out_specs=pl.BlockSpec((1, tq, w), lambda bi, i: (bi, i, 0)),
        out_shape=jax.ShapeDtypeStruct((b, s, w), BF16),
        compiler_params=pltpu.CompilerParams(
            dimension_semantics=("parallel", "parallel"),
            vmem_limit_bytes=_vmem_limit(blocks)),
        name="mem_xattn",
    )(proj, kv)


HG_CHUNK = 128


def _hgrn_levels(c):
    levels, m = [], 1
    while m < c:
        levels.append(m)
        m *= 2
    return levels


def _hgrn_constants(c):
    t = np.arange(c)[:, None]
    s = np.arange(c)[None, :]
    intervals, masks = [], []
    for m in _hgrn_levels(c):
        edge = (t // (2 * m)) * (2 * m) + m - 1
        upper = (t % (2 * m)) >= m
        intervals.append(np.where(upper, (s > edge) & (s <= t), (s > t) & (s <= edge)))
        masks.append(upper & ((s // (2 * m)) == (t // (2 * m))) & ((s % (2 * m)) < m))
    intervals.append(s <= t)
    intervals.append(s > t)
    iv = np.concatenate(intervals, 0)
    return (jnp.asarray(np.concatenate([iv, iv], 1), BF16),
            jnp.asarray(np.concatenate(masks, 0), F32))


def _hgrn_kernel(lbl_ref, ng_ref, iv_ref, mk_ref, q_ref, f_ref, i_ref, g_ref, *refs,
                 host_scales, ts, nh, layer):
    nhost = len(host_scales)
    host_in, o_ref, host_out = refs[:nhost], refs[nhost], refs[nhost + 1:2 * nhost + 1]
    st_ref = refs[2 * nhost + 1]
    c = HG_CHUNK
    kd = HG_DIM
    levels = _hgrn_levels(c)
    nl = len(levels)

    @pl.when(pl.program_id(2) == 0)
    def _():
        st_ref[...] = jnp.zeros_like(st_ref)

    _cast_hosted(host_in, host_out, host_scales)

    logits = [lbl_ref[r, 0] for r in range(lbl_ref.shape[0])]
    mx = functools.reduce(jnp.maximum, logits)
    ex = [jnp.exp(v - mx) for v in logits]
    lb = sum(ex[1:layer + 1], ex[0]) / sum(ex[1:], ex[0])
    ng = ng_ref[0]

    def head(x, hd):
        return x[:, hd * kd:(hd + 1) * kd]

    nc = ts // c
    rows = [slice(ci * c, (ci + 1) * c) for ci in range(nc)]

    qf, kk, v16, dec, a_tot = {}, {}, {}, {}, {}

    def stage1(ci):
        sl = rows[ci]
        z = f_ref[0, sl, :]
        qq = q_ref[0, sl, :]
        sig = jax.nn.sigmoid(z)
        lf2 = jnp.log(lb + (1.0 - lb) * sig) * LOG2E
        kk[ci] = (1.0 - lb) * (1.0 - sig)
        qf[ci] = qq * jax.nn.sigmoid(qq)
        v16[ci] = i_ref[0, sl, :].astype(BF16)
        hi = lf2.astype(BF16)
        lo = (lf2 - hi.astype(F32)).astype(BF16)
        parts = jnp.concatenate([hi, lo], axis=0)
        dec[ci] = [jnp.exp2(jnp.dot(iv_ref[p * c:(p + 1) * c, :], parts,
                                    preferred_element_type=F32)) for p in range(nl + 2)]

    def stage2(ci):
        acc = [jnp.zeros((c, c), F32) for _ in range(nh)]
        for li in range(nl):
            qt = (qf[ci] * dec[ci][li]).astype(BF16)
            kt = (kk[ci] * dec[ci][li]).astype(BF16)
            mask = mk_ref[li * c:(li + 1) * c, :]
            for hd in range(nh):
                acc[hd] = acc[hd] + _dot_nt(head(qt, hd), head(kt, hd)) * mask
        a_tot[ci] = [a.astype(BF16) for a in acc]

    st = [st_ref[hd] for hd in range(nh)]

    def stage3(ci):
        sl = rows[ci]
        e_b = dec[ci][nl]
        e_r = dec[ci][nl + 1]
        qb = (qf[ci] * e_b).astype(BF16)
        kr = (kk[ci] * e_r).astype(BF16)
        qk = qf[ci] * kk[ci]
        v = i_ref[0, sl, :]
        outs = []
        for hd in range(nh):
            vh = head(v16[ci], hd)
            o = jnp.sum(head(qk, hd), axis=-1, keepdims=True) * head(v, hd)
            o = o + jnp.dot(a_tot[ci][hd], vh, preferred_element_type=F32)
            o = o + _dot_nt(head(qb, hd), st[hd].astype(BF16))
            st[hd] = st[hd] * head(e_b, hd)[c - 1:c, :] + lax.dot_general(
                vh, head(kr, hd), (((0,), (0,)), ((), ())), preferred_element_type=F32)
            outs.append(o * lax.rsqrt(jnp.mean(o * o, axis=-1, keepdims=True) + EPS))
        o = jnp.concatenate(outs, axis=1) * ng * jax.nn.sigmoid(g_ref[0, sl, :])
        o_ref[0, sl, :] = o.astype(o_ref.dtype)

    for step in range(nc + 2):
        if step < nc:
            stage1(step)
        if 0 <= step - 1 < nc:
            stage2(step - 1)
        if 0 <= step - 2 < nc:
            stage3(step - 2)
    for hd in range(nh):
        st_ref[hd] = st[hd]


def _hgrn(proj, lb_logits, norm_g, *, layer, hosted=(), host_scales=(), ts=1024, nh=2):
    b, s, _ = proj.shape
    kd = HG_DIM
    c = HG_CHUNK
    w = nh * kd
    ng = HG_HEADS // nh
    ns = s // ts
    nl = lb_logits.shape[0]
    intervals, masks = _hgrn_constants(c)
    h_in, h_out, h_shapes, h_bytes = _hosted_specs(
        hosted, b * ng * ns, lambda bi, h, i: (bi * ng + h) * ns + i)
    blocks = (2 * 4 * ts * w * 4 + 2 * ts * w * 2 + nh * kd * kd * 4 + 2 * intervals.size * 2
              + 2 * masks.size * 4 + 96 * c * w * 4 + h_bytes)

    def col(off):
        return lambda bi, h, i: (bi, i, off * ng + h)

    return pl.pallas_call(
        functools.partial(_hgrn_kernel, host_scales=tuple(host_scales), ts=ts, nh=nh,
                          layer=layer),
        grid=(b, ng, ns),
        in_specs=[
            pl.BlockSpec((nl, 1, 1, w), lambda bi, h, i: (0, h, 0, 0)),
            pl.BlockSpec((1, 1, w), lambda bi, h, i: (h, 0, 0)),
            pl.BlockSpec(intervals.shape, lambda bi, h, i: (0, 0)),
            pl.BlockSpec(masks.shape, lambda bi, h, i: (0, 0)),
            pl.BlockSpec((1, ts, w), col(0)),
            pl.BlockSpec((1, ts, w), col(1)),
            pl.BlockSpec((1, ts, w), col(2)),
            pl.BlockSpec((1, ts, w), col(3)),
        ] + h_in,
        out_specs=[pl.BlockSpec((1, ts, w), lambda bi, h, i: (bi, i, h))] + h_out,
        out_shape=[jax.ShapeDtypeStruct((b, s, HG_HEADS * kd), BF16)] + h_shapes,
        scratch_shapes=[pltpu.VMEM((nh, kd, kd), F32)],
        compiler_params=pltpu.CompilerParams(
            dimension_semantics=("parallel", "parallel", "arbitrary"),
            vmem_limit_bytes=_vmem_limit(blocks)),
        name="hgrn2",
    )(lb_logits.reshape(nl, ng, 1, w), norm_g.reshape(ng, 1, w), intervals, masks,
      proj, proj, proj, proj, *hosted)


def _merge_kernel(h_hbm, g_ref, y0_ref, y1_ref, y2_ref, wg0_ref, wg1_ref, wg2_ref,
                  wb0_ref, wb1_ref, wb2_ref, wout_ref, o_ref, h_buf, u_ref, mg_ref, h_sem, o_sem):
    i = pl.program_id(0)
    j = pl.program_id(1)
    nj = mg_ref.shape[0]
    tn = mg_ref.shape[2]
    tm = h_buf.shape[0]

    def h_copy(tile):
        return pltpu.make_async_copy(h_hbm.at[pl.ds(tile * tm, tm), :], h_buf, h_sem)

    def o_init():
        return pltpu.make_async_copy(h_buf, o_ref, o_sem)

    @pl.when((j == 0) & (i == 0))
    def _():
        h_copy(0).start()

    @pl.when(j == 0)
    def _():
        h_copy(i).wait()
        u_ref[...] = _rms(h_buf[...], g_ref[...]).astype(BF16)
        o_init().start()

    @pl.when(j == 1)
    def _():
        o_init().wait()

    @pl.when((j == 1) & (i + 1 < pl.num_programs(0)))
    def _():
        h_copy(i + 1).start()

    u = u_ref[...]
    merged = None
    for y_ref, wg_ref, wb_ref in ((y0_ref, wg0_ref, wb0_ref), (y1_ref, wg1_ref, wb1_ref),
                                  (y2_ref, wg2_ref, wb2_ref)):
        gate = jax.nn.sigmoid(jnp.dot(u, wg_ref[...], preferred_element_type=F32))
        term = gate * jnp.dot(y_ref[...], wb_ref[...], preferred_element_type=F32)
        merged = term if merged is None else merged + term
    mg_ref[j] = merged.astype(BF16)

    @pl.when(j == nj - 1)
    def _():
        acc = o_ref[...]
        for jj in range(nj):
            acc = acc + jnp.dot(mg_ref[jj], wout_ref[jj * tn:(jj + 1) * tn, :],
                                preferred_element_type=F32)
        o_ref[...] = acc


def _merge(h, gain, y0, y1, y2, w_in, gate_col0, wb0, wb1, wb2, wout, *, tm=512, tn=512):
    t, d = h.shape
    w = y0.shape[1]
    nj = d // tn
    assert nj >= 2
    blocks = (3 * tm * d * 4 + tm * d * 2 + tm * d * 2 + 2 * 3 * tm * w * 2
              + 2 * 3 * d * tn * 2 + 2 * 3 * w * tn * 2 + d * d * 2 + 8 * tm * tn * 4)

    def gate_spec(br):
        off = (gate_col0 + br * d) // tn
        return pl.BlockSpec((d, tn), lambda i, j: (0, off + j))

    y_spec = pl.BlockSpec((tm, w), lambda i, j: (i, 0))
    wb_spec = pl.BlockSpec((w, tn), lambda i, j: (0, j))
    return pl.pallas_call(
        _merge_kernel,
        grid=(t // tm, nj),
        in_specs=[
            pl.BlockSpec(memory_space=pl.ANY),
            pl.BlockSpec((1, d), lambda i, j: (0, 0)),
            y_spec, y_spec, y_spec,
            gate_spec(0), gate_spec(1), gate_spec(2),
            wb_spec, wb_spec, wb_spec,
            pl.BlockSpec((d, d), lambda i, j: (0, 0), pipeline_mode=pl.Buffered(1)),
        ],
        out_specs=pl.BlockSpec((tm, d), lambda i, j: (i, 0)),
        out_shape=jax.ShapeDtypeStruct((t, d), F32),
        scratch_shapes=[pltpu.VMEM((tm, d), F32), pltpu.VMEM((tm, d), BF16),
                        pltpu.VMEM((nj, tm, tn), BF16), pltpu.SemaphoreType.DMA(()),
                        pltpu.SemaphoreType.DMA(())],
        compiler_params=pltpu.CompilerParams(
            dimension_semantics=("arbitrary", "arbitrary"),
            vmem_limit_bytes=_vmem_limit(blocks)),
        name="merge_out",
    )(h, gain.reshape(1, d), y0, y1, y2, w_in, w_in, w_in, wb0, wb1, wb2, wout)


def kernel(x, mem, ffn1_norm, ffn1_w_gate, ffn1_w_up, ffn1_w_down, mix_norm, mem_norm, w_in,
           da_lambda_q1, da_lambda_k1, da_lambda_q2, da_lambda_k2, da_subln, hg_lb_logits, hg_norm,
           w_mem_kv, w_branch_da, w_branch_hg, w_branch_xa, w_out,
           ffn2_norm, ffn2_w_gate, ffn2_w_up, ffn2_w_down, final_norm):
    bsz, seq, d = x.shape
    depth = w_in.shape[0]
    t = bsz * seq
    da_w = DA_HEADS * 2 * DA_HEAD_DIM
    hg_w = HG_HEADS * HG_DIM
    xa_w = XA_HEADS * XA_HEAD_DIM
    slopes = jnp.asarray([2.0 ** (-8.0 * (i + 1) / DA_HEADS) for i in range(DA_HEADS)], F32)

    h = x.reshape(t, d)
    for l in range(depth):
        lambda_init = 0.8 - 0.6 * math.exp(-0.3 * l)
        h, w_in16 = _ffn(h, ffn1_norm[l], ffn1_w_gate[l], ffn1_w_up[l], ffn1_w_down[l],
                         final_norm, wd_scale=0.5, final_norm=False,
                         hosted=(w_in[l],), host_scales=(1.0,), tf=256)
        p_att, p_hg = _in_proj(h, mix_norm[l], w_in16, da_cols=3 * da_w, hg_cols=4 * hg_w,
                               xa_cols=xa_w)
        p_att = p_att.reshape(bsz, seq, -1)
        p_hg = p_hg.reshape(bsz, seq, -1)

        lam4 = jnp.stack([da_lambda_q1[l], da_lambda_k1[l], da_lambda_q2[l], da_lambda_k2[l]])
        y_da, wg2, wu2, wd2_half = _diff_attn(
            p_att, slopes, lam4, da_subln[l], lambda_init=lambda_init,
            hosted=(ffn2_w_gate[l], ffn2_w_up[l], ffn2_w_down[l]), host_scales=(1.0, 1.0, 0.5))
        y_hg, wb_da, wb_hg, wb_xa, wo = _hgrn(
            p_hg, hg_lb_logits, hg_norm[l], layer=l,
            hosted=(w_branch_da[l], w_branch_hg[l], w_branch_xa[l], w_out[l]),
            host_scales=(1.0, 1.0, 1.0, 1.0))
        kv = _norm_matmul(mem.reshape(-1, d), mem_norm[l], w_mem_kv[l].astype(BF16), BF16,
                          tm=512, name="mem_kv").reshape(bsz, mem.shape[1], -1)
        y_xa = _xattn(p_att, kv, q_col_block=3 * da_w // xa_w)

        h = _merge(h, mix_norm[l], y_da.reshape(t, da_w), y_hg.reshape(t, hg_w),
                   y_xa.reshape(t, xa_w), w_in16, 3 * da_w + 4 * hg_w + xa_w,
                   wb_da, wb_hg, wb_xa, wo)

        h, = _ffn(h, ffn2_norm[l], wg2, wu2, wd2_half, final_norm, wd_scale=1.0,
                  final_norm=(l == depth - 1))
    return h.reshape(bsz, seq, d)
```

```python
import functools
import math

import numpy as np
import jax
import jax.numpy as jnp
from jax import lax
from jax.experimental import pallas as pl
from jax.experimental.pallas import tpu as pltpu

F32 = jnp.float32
BF16 = jnp.bfloat16

EPS = 1e-6
DA_HEADS = 4
DA_HEAD_DIM = 128
HG_HEADS = 8
HG_DIM = 128
XA_HEADS = 4
XA_HEAD_DIM = 256
LOG2E = math.log2(math.e)
PROJ_TN = 1024

V7X_VMEM_BYTES = 64 * 1024 * 1024
V7X_LANES = 128

NEG_BIG = -1e30


def _vmem_limit(block_bytes):
    want = int(block_bytes * 1.25) + (4 << 20)
    return min(want, V7X_VMEM_BYTES - (4 << 20))


def _rms(x, g):
    return x * lax.rsqrt(jnp.mean(x * x, axis=-1, keepdims=True) + EPS) * g


COPY_ROWS = 64


def _copy_rows(src_ref, dst_ref):
    def rows(r, carry):
        sl = pl.ds(pl.multiple_of(r * COPY_ROWS, COPY_ROWS), COPY_ROWS)
        dst_ref[sl, :] = src_ref[sl, :]
        return carry

    lax.fori_loop(0, src_ref.shape[0] // COPY_ROWS, rows, 0)


def _dot_nt(a, b):
    return lax.dot_general(a, b, (((1,), (1,)), ((), ())), preferred_element_type=F32)


def _ffn_kernel(x_hbm, g_ref, wg_ref, wu_ref, wd_ref, fg_ref, *refs, host_scales, final_norm):
    nhost = len(host_scales)
    host_in, o_ref, host_out = refs[:nhost], refs[nhost], refs[nhost + 1:2 * nhost + 1]
    x_buf, xn_ref, x_sem = refs[2 * nhost + 1:]
    i = pl.program_id(0)
    j = pl.program_id(1)
    tm = x_buf.shape[0]

    def x_copy(tile):
        return pltpu.make_async_copy(x_hbm.at[pl.ds(tile * tm, tm), :], x_buf, x_sem)

    @pl.when((j == 0) & (i == 0))
    def _():
        x_copy(0).start()

    @pl.when(j == 0)
    def _():
        x_copy(i).wait()
        xn_ref[...] = _rms(x_buf[...], g_ref[...]).astype(BF16)
        _copy_rows(x_buf, o_ref)

    @pl.when((j == 1) & (i + 1 < pl.num_programs(0)))
    def _():
        x_copy(i + 1).start()

    _cast_hosted(host_in, host_out, host_scales)
    xn = xn_ref[...]
    a = jnp.dot(xn, wg_ref[...], preferred_element_type=F32)
    b = jnp.dot(xn, wu_ref[...], preferred_element_type=F32)
    hmid = (a * jax.nn.sigmoid(a) * b).astype(BF16)
    o_ref[...] += jnp.dot(hmid, wd_ref[...], preferred_element_type=F32)

    if final_norm:
        @pl.when(j == pl.num_programs(1) - 1)
        def _():
            o_ref[...] = _rms(o_ref[...], fg_ref[...])


FFN_HOST_STEPS = 8


def _ffn(x, gain, wg, wu, wd_half, final_gain, *, final_norm, hosted=(), host_scales=(),
         tm=1024, tf=512):
    wd = wd_half
    t, d = x.shape
    dff = wg.shape[1]
    nf = dff // tf
    assert nf >= max(2, FFN_HOST_STEPS)
    h_in, h_out, h_shapes, h_bytes = _hosted_specs(
        hosted, (t // tm) * FFN_HOST_STEPS,
        lambda i, j: i * FFN_HOST_STEPS + jnp.minimum(j, FFN_HOST_STEPS - 1))
    blocks = 3 * tm * d * 4 + tm * d * 2 + 2 * 3 * d * tf * 2 + 3 * tm * tf * 4 + h_bytes
    return pl.pallas_call(
        functools.partial(_ffn_kernel, host_scales=tuple(host_scales), final_norm=final_norm),
        grid=(t // tm, nf),
        in_specs=[
            pl.BlockSpec(memory_space=pl.ANY),
            pl.BlockSpec((1, d), lambda i, j: (0, 0)),
            pl.BlockSpec((d, tf), lambda i, j: (0, j)),
            pl.BlockSpec((d, tf), lambda i, j: (0, j)),
            pl.BlockSpec((tf, d), lambda i, j: (j, 0)),
            pl.BlockSpec((1, d), lambda i, j: (0, 0)),
        ] + h_in,
        out_specs=[pl.BlockSpec((tm, d), lambda i, j: (i, 0))] + h_out,
        out_shape=[jax.ShapeDtypeStruct((t, d), F32)] + h_shapes,
        scratch_shapes=[pltpu.VMEM((tm, d), F32), pltpu.VMEM((tm, d), BF16),
                        pltpu.SemaphoreType.DMA(())],
        compiler_params=pltpu.CompilerParams(
            dimension_semantics=("arbitrary", "arbitrary"),
            vmem_limit_bytes=_vmem_limit(blocks)),
        name="ffn_final" if final_norm else "ffn",
    )(x, gain.reshape(1, d), wg, wu, wd, final_gain.reshape(1, d), *hosted)


def _norm_matmul_kernel(x_ref, g_ref, w_ref, o_ref, xn_ref):
    @pl.when(pl.program_id(1) == 0)
    def _():
        xn_ref[...] = _rms(x_ref[...], g_ref[...]).astype(BF16)

    o_ref[...] = jnp.dot(xn_ref[...], w_ref[...], preferred_element_type=F32).astype(o_ref.dtype)


def _norm_matmul(x, gain, w, out_dtype, *, tm, tn=PROJ_TN, name):
    t, d = x.shape
    n_cols = w.shape[1]
    out_bytes = jnp.dtype(out_dtype).itemsize
    blocks = (2 * tm * d * 4 + tm * d * 2 + 2 * d * tn * 2 + 2 * tm * tn * out_bytes
              + tm * tn * 4)
    return pl.pallas_call(
        _norm_matmul_kernel,
        grid=(t // tm, n_cols // tn),
        in_specs=[
            pl.BlockSpec((tm, d), lambda i, j: (i, 0)),
            pl.BlockSpec((1, d), lambda i, j: (0, 0)),
            pl.BlockSpec((d, tn), lambda i, j: (0, j)),
        ],
        out_specs=pl.BlockSpec((tm, tn), lambda i, j: (i, j)),
        out_shape=jax.ShapeDtypeStruct((t, n_cols), out_dtype),
        scratch_shapes=[pltpu.VMEM((tm, d), BF16)],
        compiler_params=pltpu.CompilerParams(
            dimension_semantics=("parallel", "arbitrary"),
            vmem_limit_bytes=_vmem_limit(blocks)),
        name=name,
    )(x, gain.reshape(1, d), w)


def _in_proj_kernel(x_ref, g_ref, w_ref, att_ref, hg_ref, xn_ref, *, n_hg):
    j = pl.program_id(1)

    @pl.when(j == 0)
    def _():
        xn_ref[...] = _rms(x_ref[...], g_ref[...]).astype(BF16)

    res = jnp.dot(xn_ref[...], w_ref[...], preferred_element_type=F32)
    att_ref[...] = res.astype(att_ref.dtype)

    @pl.when(j < n_hg)
    def _():
        hg_ref[...] = res


def _in_proj(x, gain, w_in, *, da_cols, hg_cols, xa_cols, tm=1024, tn=PROJ_TN):
    t, d = x.shape
    da_b, hg_b, xa_b = da_cols // tn, hg_cols // tn, xa_cols // tn

    def w_col(j):
        a = j - hg_b
        return jnp.where(j < hg_b, da_b + j, jnp.where(a < da_b, a, a + hg_b))

    blocks = (2 * tm * d * 4 + tm * d * 2 + 2 * d * tn * 2 + 2 * tm * tn * 2 + 2 * tm * tn * 4
              + tm * tn * 4)
    return pl.pallas_call(
        functools.partial(_in_proj_kernel, n_hg=hg_b),
        grid=(t // tm, hg_b + da_b + xa_b),
        in_specs=[
            pl.BlockSpec((tm, d), lambda i, j: (i, 0)),
            pl.BlockSpec((1, d), lambda i, j: (0, 0)),
            pl.BlockSpec((d, tn), lambda i, j: (0, w_col(j))),
        ],
        out_specs=[
            pl.BlockSpec((tm, tn), lambda i, j: (i, jnp.maximum(j - hg_b, 0))),
            pl.BlockSpec((tm, tn), lambda i, j: (i, jnp.minimum(j, hg_b - 1))),
        ],
        out_shape=[jax.ShapeDtypeStruct((t, da_cols + xa_cols), BF16),
                   jax.ShapeDtypeStruct((t, hg_cols), F32)],
        scratch_shapes=[pltpu.VMEM((tm, d), BF16)],
        compiler_params=pltpu.CompilerParams(
            dimension_semantics=("parallel", "arbitrary"),
            vmem_limit_bytes=_vmem_limit(blocks)),
        name="in_proj",
    )(x, gain.reshape(1, d), w_in)


BF16_SUBLANES = 16


def _hosted_specs(weights, n_steps, step_index):
    in_specs, out_specs, out_shapes, nbytes = [], [], [], 0
    for w in weights:
        rows, cols = w.shape
        slab = rows // n_steps
        assert rows % n_steps == 0 and slab % BF16_SUBLANES == 0
        spec = pl.BlockSpec((slab, cols), lambda *g: (step_index(*g), 0))
        in_specs.append(spec)
        out_specs.append(spec)
        out_shapes.append(jax.ShapeDtypeStruct(w.shape, BF16))
        nbytes += 2 * slab * cols * (4 + 2)
    return in_specs, out_specs, out_shapes, nbytes


def _cast_hosted(src_refs, dst_refs, scales):
    for src, dst, scale in zip(src_refs, dst_refs, scales):
        x = src[...]
        dst[...] = (x if scale == 1.0 else x * scale).astype(BF16)


def _diff_attn_kernel(slope_ref, lam_ref, sg_ref, q_ref, k_ref, v_ref, *refs,
                      host_scales, tq, tr, tk, lambda_init):
    nhost = len(host_scales)
    host_in, o_ref, host_out = refs[:nhost], refs[nhost], refs[nhost + 1:2 * nhost + 1]
    m_ref, l_ref, acc_ref = refs[2 * nhost + 1:]
    h = pl.program_id(1)
    qi = pl.program_id(2)
    d = DA_HEAD_DIM
    hd = 2 * d
    nr = tq // tr
    scale2 = d ** -0.5 * LOG2E
    slope2 = slope_ref[h] * LOG2E

    m_ref[...] = jnp.full_like(m_ref, NEG_BIG)
    l_ref[...] = jnp.zeros_like(l_ref)
    acc_ref[...] = jnp.zeros_like(acc_ref)

    def lanes(x, width):
        return jnp.concatenate([x] * (width // V7X_LANES), axis=1)

    def tiles(row_tiles, k_start, width, masked_r):
        kb = k_ref[0, pl.ds(k_start, width), :]
        vb = v_ref[0, pl.ds(k_start, width), :]
        col = lax.broadcasted_iota(jnp.int32, (1, width), 1)
        chains = [(r, c) for r in row_tiles for c in range(2)]
        s_all = {}
        for r, c in chains:
            bias = slope2 * (col + (k_start - (qi * tq + r * tr))).astype(F32)
            q = q_ref[0, r * tr:(r + 1) * tr, c * d:(c + 1) * d]
            s = _dot_nt(q, kb[:, c * d:(c + 1) * d]) * scale2 + bias
            if r == masked_r:
                keep = (lax.broadcasted_iota(jnp.int32, (tr, width), 0)
                        >= lax.broadcasted_iota(jnp.int32, (tr, width), 1))
                s = jnp.where(keep, s, NEG_BIG)
            s_all[r, c] = s
        p_all, alpha_all = {}, {}
        for r, c in chains:
            idx = 2 * r + c
            s = s_all[r, c]
            m_old = m_ref[idx]
            m_new = jnp.maximum(m_old, jnp.max(s, axis=-1, keepdims=True))
            alpha = jnp.exp2(m_old - m_new)
            p = jnp.exp2(s - lanes(m_new, width))
            l_ref[idx] = alpha * l_ref[idx] + jnp.sum(p, axis=-1, keepdims=True)
            m_ref[idx] = m_new
            p_all[r, c] = p.astype(BF16)
            alpha_all[r, c] = alpha
        for r, c in chains:
            idx = 2 * r + c
            acc_ref[idx] = lanes(alpha_all[r, c], hd) * acc_ref[idx] + jnp.dot(
                p_all[r, c], vb, preferred_element_type=F32)

    def body(ki, carry):
        tiles(range(nr), pl.multiple_of(ki * tk, tk), tk, None)
        return carry

    lax.fori_loop(0, qi * (tq // tk), body, 0)
    _cast_hosted(host_in, host_out, host_scales)
    for cc in range(nr):
        tiles(range(cc, nr), pl.multiple_of(qi * tq + cc * tr, tr), tr, cc)

    lam4 = lam_ref[...]
    lam = (jnp.exp(jnp.sum(lam4[0:1] * lam4[1:2], axis=-1, keepdims=True))
           - jnp.exp(jnp.sum(lam4[2:3] * lam4[3:4], axis=-1, keepdims=True)) + lambda_init)
    for r in range(nr):
        o = (acc_ref[2 * r] / lanes(l_ref[2 * r], hd)
             - lam * (acc_ref[2 * r + 1] / lanes(l_ref[2 * r + 1], hd)))
        o = _rms(o, sg_ref[...]) * (1.0 - lambda_init)
        o_ref[0, r * tr:(r + 1) * tr, :] = o.astype(o_ref.dtype)


def _diff_attn(proj, slopes, lam4, subln_g, *, lambda_init, hosted=(), host_scales=(),
               tq=2048, tr=512, tk=512):
    b, s, _ = proj.shape
    hd = 2 * DA_HEAD_DIM
    nchain = 2 * (tq // tr)
    nq = s // tq
    h_in, h_out, h_shapes, h_bytes = _hosted_specs(
        hosted, b * DA_HEADS * nq, lambda bi, h, i: (bi * DA_HEADS + h) * nq + i)
    blocks = (2 * tq * hd * 2 + 2 * 2 * s * hd * 2 + 2 * tq * hd * 2
              + nchain * tr * (hd + 2 * V7X_LANES) * 4 + 4 * nchain * tr * tk * 4 + h_bytes)
    return pl.pallas_call(
        functools.partial(_diff_attn_kernel, host_scales=tuple(host_scales), tq=tq, tr=tr, tk=tk,
                          lambda_init=lambda_init),
        grid=(b, DA_HEADS, nq),
        in_specs=[
            pl.BlockSpec(memory_space=pltpu.SMEM),
            pl.BlockSpec((4, DA_HEAD_DIM), lambda bi, h, i: (0, 0)),
            pl.BlockSpec((1, hd), lambda bi, h, i: (0, 0)),
            pl.BlockSpec((1, tq, hd), lambda bi, h, i: (bi, i, h)),
            pl.BlockSpec((1, s, hd), lambda bi, h, i: (bi, 0, DA_HEADS + h)),
            pl.BlockSpec((1, s, hd), lambda bi, h, i: (bi, 0, 2 * DA_HEADS + h)),
        ] + h_in,
        out_specs=[pl.BlockSpec((1, tq, hd), lambda bi, h, i: (bi, i, h))] + h_out,
        out_shape=[jax.ShapeDtypeStruct((b, s, DA_HEADS * hd), BF16)] + h_shapes,
        scratch_shapes=[pltpu.VMEM((nchain, tr, V7X_LANES), F32),
                        pltpu.VMEM((nchain, tr, V7X_LANES), F32),
                        pltpu.VMEM((nchain, tr, hd), F32)],
        compiler_params=pltpu.CompilerParams(
            dimension_semantics=("parallel", "parallel", "arbitrary"),
            vmem_limit_bytes=_vmem_limit(blocks)),
        name="diff_attn",
    )(slopes, lam4, subln_g.reshape(1, hd), proj, proj, proj, *hosted)


def _xattn_kernel(q_ref, kv_ref, o_ref):
    hd = XA_HEAD_DIM
    for h in range(XA_HEADS):
        q = q_ref[0, :, h * hd:(h + 1) * hd]
        k = kv_ref[0, :, h * hd:(h + 1) * hd]
        v = kv_ref[0, :, (XA_HEADS + h) * hd:(XA_HEADS + h + 1) * hd]
        s = _dot_nt(q, k) * (hd ** -0.5)
        p = jnp.exp(s - jnp.max(s, axis=-1, keepdims=True))
        p = p / jnp.sum(p, axis=-1, keepdims=True)
        o_ref[0, :, h * hd:(h + 1) * hd] = jnp.dot(
            p.astype(BF16), v, preferred_element_type=F32).astype(o_ref.dtype)


def _xattn(proj, kv, *, q_col_block, tq=1024):
    b, s, _ = proj.shape
    m = kv.shape[1]
    w = XA_HEADS * XA_HEAD_DIM
    blocks = 2 * 2 * tq * w * 2 + 2 * m * 2 * w * 2 + 4 * 4 * tq * m * 4
    return pl.pallas_call(
        _xattn_kernel,
        grid=(b, s // tq),
        in_specs=[
            pl.BlockSpec((1, tq, w), lambda bi, i: (bi, i, q_col_block)),
            pl.BlockSpec((1, m, 2 * w), lambda bi, i: (bi, 0, 0)),
        ],
        out_specs=pl.BlockSpec((1, tq, w), lambda bi, i: (bi, i, 0)),
        out_shape=jax.ShapeDtypeStruct((b, s, w), BF16),
        compiler_params=pltpu.CompilerParams(
            dimension_semantics=("parallel", "parallel"),
            vmem_limit_bytes=_vmem_limit(blocks)),
        name="mem_xattn",
    )(proj, kv)


HG_CHUNK = 128


def _hgrn_levels(c):
    levels, m = [], 1
    while m < c:
        levels.append(m)
        m *= 2
    return levels


def _hgrn_constants(c):
    t = np.arange(c)[:, None]
    s = np.arange(c)[None, :]
    intervals, masks = [], []
    for m in _hgrn_levels(c):
        edge = (t // (2 * m)) * (2 * m) + m - 1
        upper = (t % (2 * m)) >= m
        intervals.append(np.where(upper, (s > edge) & (s <= t), (s > t) & (s <= edge)))
        masks.append(upper & ((s // (2 * m)) == (t // (2 * m))) & ((s % (2 * m)) < m))
    intervals.append(s <= t)
    intervals.append(s > t)
    iv = np.concatenate(intervals, 0)
    return (jnp.asarray(np.concatenate([iv, iv], 1), BF16),
            jnp.asarray(np.concatenate(masks, 0), F32))


def _hgrn_kernel(lbl_ref, ng_ref, iv_ref, mk_ref, q_ref, f_ref, i_ref, g_ref, *refs,
                 host_scales, ts, nh, layer):
    nhost = len(host_scales)
    host_in, o_ref, host_out = refs[:nhost], refs[nhost], refs[nhost + 1:2 * nhost + 1]
    st_ref = refs[2 * nhost + 1]
    c = HG_CHUNK
    kd = HG_DIM
    levels = _hgrn_levels(c)
    nl = len(levels)

    @pl.when(pl.program_id(2) == 0)
    def _():
        st_ref[...] = jnp.zeros_like(st_ref)

    _cast_hosted(host_in, host_out, host_scales)

    logits = [lbl_ref[r, 0] for r in range(lbl_ref.shape[0])]
    mx = functools.reduce(jnp.maximum, logits)
    ex = [jnp.exp(v - mx) for v in logits]
    lb = sum(ex[1:layer + 1], ex[0]) / sum(ex[1:], ex[0])
    ng = ng_ref[0]

    def head(x, hd):
        return x[:, hd * kd:(hd + 1) * kd]

    nc = ts // c
    rows = [slice(ci * c, (ci + 1) * c) for ci in range(nc)]

    qf, kk, v16, dec, a_tot = {}, {}, {}, {}, {}

    def stage1(ci):
        sl = rows[ci]
        z = f_ref[0, sl, :]
        qq = q_ref[0, sl, :]
        sig = jax.nn.sigmoid(z)
        lf2 = jnp.log(lb + (1.0 - lb) * sig) * LOG2E
        kk[ci] = (1.0 - lb) * (1.0 - sig)
        qf[ci] = qq * jax.nn.sigmoid(qq)
        v16[ci] = i_ref[0, sl, :].astype(BF16)
        hi = lf2.astype(BF16)
        lo = (lf2 - hi.astype(F32)).astype(BF16)
        parts = jnp.concatenate([hi, lo], axis=0)
        dec[ci] = [jnp.exp2(jnp.dot(iv_ref[p * c:(p + 1) * c, :], parts,
                                    preferred_element_type=F32)) for p in range(nl + 2)]

    def stage2(ci):
        acc = [jnp.zeros((c, c), F32) for _ in range(nh)]
        for li in range(nl):
            qt = (qf[ci] * dec[ci][li]).astype(BF16)
            kt = (kk[ci] * dec[ci][li]).astype(BF16)
            mask = mk_ref[li * c:(li + 1) * c, :]
            for hd in range(nh):
                acc[hd] = acc[hd] + _dot_nt(head(qt, hd), head(kt, hd)) * mask
        a_tot[ci] = [a.astype(BF16) for a in acc]

    st = [st_ref[hd] for hd in range(nh)]

    def stage3(ci):
        sl = rows[ci]
        e_b = dec[ci][nl]
        e_r = dec[ci][nl + 1]
        qb = (qf[ci] * e_b).astype(BF16)
        kr = (kk[ci] * e_r).astype(BF16)
        qk = qf[ci] * kk[ci]
        v = i_ref[0, sl, :]
        outs = []
        for hd in range(nh):
            vh = head(v16[ci], hd)
            o = jnp.sum(head(qk, hd), axis=-1, keepdims=True) * head(v, hd)
            o = o + jnp.dot(a_tot[ci][hd], vh, preferred_element_type=F32)
            o = o + _dot_nt(head(qb, hd), st[hd].astype(BF16))
            st[hd] = st[hd] * head(e_b, hd)[c - 1:c, :] + lax.dot_general(
                vh, head(kr, hd), (((0,), (0,)), ((), ())), preferred_element_type=F32)
            outs.append(o * lax.rsqrt(jnp.mean(o * o, axis=-1, keepdims=True) + EPS))
        o = jnp.concatenate(outs, axis=1) * ng * jax.nn.sigmoid(g_ref[0, sl, :])
        o_ref[0, sl, :] = o.astype(o_ref.dtype)

    for step in range(nc + 2):
        if 0 <= step - 2 < nc:
            stage3(step - 2)
        if 0 <= step - 1 < nc:
            stage2(step - 1)
        if step < nc:
            stage1(step)
    for hd in range(nh):
        st_ref[hd] = st[hd]


def _hgrn(proj, lb_logits, norm_g, *, layer, hosted=(), host_scales=(), ts=1024, nh=2):
    b, s, _ = proj.shape
    kd = HG_DIM
    c = HG_CHUNK
    w = nh * kd
    ng = HG_HEADS // nh
    ns = s // ts
    nl = lb_logits.shape[0]
    intervals, masks = _hgrn_constants(c)
    h_in, h_out, h_shapes, h_bytes = _hosted_specs(
        hosted, b * ng * ns, lambda bi, h, i: (bi * ng + h) * ns + i)
    blocks = (2 * 4 * ts * w * 4 + 2 * ts * w * 2 + nh * kd * kd * 4 + 2 * intervals.size * 2
              + 2 * masks.size * 4 + 96 * c * w * 4 + h_bytes)

    def col(off):
        return lambda bi, h, i: (bi, i, off * ng + h)

    return pl.pallas_call(
        functools.partial(_hgrn_kernel, host_scales=tuple(host_scales), ts=ts, nh=nh,
                          layer=layer),
        grid=(b, ng, ns),
        in_specs=[
            pl.BlockSpec((nl, 1, 1, w), lambda bi, h, i: (0, h, 0, 0)),
            pl.BlockSpec((1, 1, w), lambda bi, h, i: (h, 0, 0)),
            pl.BlockSpec(intervals.shape, lambda bi, h, i: (0, 0)),
            pl.BlockSpec(masks.shape, lambda bi, h, i: (0, 0)),
            pl.BlockSpec((1, ts, w), col(0)),
            pl.BlockSpec((1, ts, w), col(1)),
            pl.BlockSpec((1, ts, w), col(2)),
            pl.BlockSpec((1, ts, w), col(3)),
        ] + h_in,
        out_specs=[pl.BlockSpec((1, ts, w), lambda bi, h, i: (bi, i, h))] + h_out,
        out_shape=[jax.ShapeDtypeStruct((b, s, HG_HEADS * kd), BF16)] + h_shapes,
        scratch_shapes=[pltpu.VMEM((nh, kd, kd), F32)],
        compiler_params=pltpu.CompilerParams(
            dimension_semantics=("parallel", "parallel", "arbitrary"),
            vmem_limit_bytes=_vmem_limit(blocks)),
        name="hgrn2",
    )(lb_logits.reshape(nl, ng, 1, w), norm_g.reshape(ng, 1, w), intervals, masks,
      proj, proj, proj, proj, *hosted)


def _merge_kernel(h_hbm, g_ref, y0_ref, y1_ref, y2_ref, wg0_ref, wg1_ref, wg2_ref,
                  wb0_ref, wb1_ref, wb2_ref, wout_ref, o_ref, h_buf, u_ref, mg_ref, h_sem, o_sem):
    i = pl.program_id(0)
    j = pl.program_id(1)
    nj = mg_ref.shape[0]
    tn = mg_ref.shape[2]
    tm = h_buf.shape[0]

    def h_copy(tile):
        return pltpu.make_async_copy(h_hbm.at[pl.ds(tile * tm, tm), :], h_buf, h_sem)

    def o_init():
        return pltpu.make_async_copy(h_buf, o_ref, o_sem)

    @pl.when((j == 0) & (i == 0))
    def _():
        h_copy(0).start()

    @pl.when(j == 0)
    def _():
        h_copy(i).wait()
        u_ref[...] = _rms(h_buf[...], g_ref[...]).astype(BF16)
        o_init().start()

    @pl.when(j == 1)
    def _():
        o_init().wait()

    @pl.when((j == 1) & (i + 1 < pl.num_programs(0)))
    def _():
        h_copy(i + 1).start()

    u = u_ref[...]
    merged = None
    for y_ref, wg_ref, wb_ref in ((y0_ref, wg0_ref, wb0_ref), (y1_ref, wg1_ref, wb1_ref),
                                  (y2_ref, wg2_ref, wb2_ref)):
        gate = jax.nn.sigmoid(jnp.dot(u, wg_ref[...], preferred_element_type=F32))
        term = gate * jnp.dot(y_ref[...], wb_ref[...], preferred_element_type=F32)
        merged = term if merged is None else merged + term
    mg_ref[j] = merged.astype(BF16)

    @pl.when(j == nj - 1)
    def _():
        acc = o_ref[...]
        for jj in range(nj):
            acc = acc + jnp.dot(mg_ref[jj], wout_ref[jj * tn:(jj + 1) * tn, :],
                                preferred_element_type=F32)
        o_ref[...] = acc


def _merge(h, gain, y0, y1, y2, w_in, gate_col0, wb0, wb1, wb2, wout, *, tm=512, tn=512):
    t, d = h.shape
    w = y0.shape[1]
    nj = d // tn
    assert nj >= 2
    blocks = (3 * tm * d * 4 + tm * d * 2 + tm * d * 2 + 2 * 3 * tm * w * 2
              + 2 * 3 * d * tn * 2 + 2 * 3 * w * tn * 2 + d * d * 2 + 8 * tm * tn * 4)

    def gate_spec(br):
        off = (gate_col0 + br * d) // tn
        return pl.BlockSpec((d, tn), lambda i, j: (0, off + j))

    y_spec = pl.BlockSpec((tm, w), lambda i, j: (i, 0))
    wb_spec = pl.BlockSpec((w, tn), lambda i, j: (0, j))
    return pl.pallas_call(
        _merge_kernel,
        grid=(t // tm, nj),
        in_specs=[
            pl.BlockSpec(memory_space=pl.ANY),
            pl.BlockSpec((1, d), lambda i, j: (0, 0)),
            y_spec, y_spec, y_spec,
            gate_spec(0), gate_spec(1), gate_spec(2),
            wb_spec, wb_spec, wb_spec,
            pl.BlockSpec((d, d), lambda i, j: (0, 0), pipeline_mode=pl.Buffered(1)),
        ],
        out_specs=pl.BlockSpec((tm, d), lambda i, j: (i, 0)),
        out_shape=jax.ShapeDtypeStruct((t, d), F32),
        scratch_shapes=[pltpu.VMEM((tm, d), F32), pltpu.VMEM((tm, d), BF16),
                        pltpu.VMEM((nj, tm, tn), BF16), pltpu.SemaphoreType.DMA(()),
                        pltpu.SemaphoreType.DMA(())],
        compiler_params=pltpu.CompilerParams(
            dimension_semantics=("arbitrary", "arbitrary"),
            vmem_limit_bytes=_vmem_limit(blocks)),
        name="merge_out",
    )(h, gain.reshape(1, d), y0, y1, y2, w_in, w_in, w_in, wb0, wb1, wb2, wout)


def kernel(x, mem, ffn1_norm, ffn1_w_gate, ffn1_w_up, ffn1_w_down, mix_norm, mem_norm, w_in,
           da_lambda_q1, da_lambda_k1, da_lambda_q2, da_lambda_k2, da_subln, hg_lb_logits, hg_norm,
           w_mem_kv, w_branch_da, w_branch_hg, w_branch_xa, w_out,
           ffn2_norm, ffn2_w_gate, ffn2_w_up, ffn2_w_down, final_norm):
    bsz, seq, d = x.shape
    depth = w_in.shape[0]
    t = bsz * seq
    da_w = DA_HEADS * 2 * DA_HEAD_DIM
    hg_w = HG_HEADS * HG_DIM
    xa_w = XA_HEADS * XA_HEAD_DIM
    slopes = jnp.asarray([2.0 ** (-8.0 * (i + 1) / DA_HEADS) for i in range(DA_HEADS)], F32)

    h = x.reshape(t, d)
    for l in range(depth):
        lambda_init = 0.8 - 0.6 * math.exp(-0.3 * l)
        h, w_in16, w_kv16 = _ffn(
            h, ffn1_norm[l], ffn1_w_gate[l].astype(BF16), ffn1_w_up[l].astype(BF16),
            (0.5 * ffn1_w_down[l]).astype(BF16), final_norm, final_norm=False,
            hosted=(w_in[l], w_mem_kv[l]), host_scales=(1.0, 1.0))
        p_att, p_hg = _in_proj(h, mix_norm[l], w_in16, da_cols=3 * da_w, hg_cols=4 * hg_w,
                               xa_cols=xa_w)
        p_att = p_att.reshape(bsz, seq, -1)
        p_hg = p_hg.reshape(bsz, seq, -1)

        lam4 = jnp.stack([da_lambda_q1[l], da_lambda_k1[l], da_lambda_q2[l], da_lambda_k2[l]])
        y_da, wg2, wu2, wd2_half = _diff_attn(
            p_att, slopes, lam4, da_subln[l], lambda_init=lambda_init,
            hosted=(ffn2_w_gate[l], ffn2_w_up[l], ffn2_w_down[l]), host_scales=(1.0, 1.0, 0.5))
        y_hg, wb_da, wb_hg, wb_xa, wo = _hgrn(
            p_hg, hg_lb_logits, hg_norm[l], layer=l,
            hosted=(w_branch_da[l], w_branch_hg[l], w_branch_xa[l], w_out[l]),
            host_scales=(1.0, 1.0, 1.0, 1.0))
        kv = _norm_matmul(mem.reshape(-1, d), mem_norm[l], w_kv16, BF16,
                          tm=512, name="mem_kv").reshape(bsz, mem.shape[1], -1)
        y_xa = _xattn(p_att, kv, q_col_block=3 * da_w // xa_w)

        h = _merge(h, mix_norm[l], y_da.reshape(t, da_w), y_hg.reshape(t, hg_w),
                   y_xa.reshape(t, xa_w), w_in16, 3 * da_w + 4 * hg_w + xa_w,
                   wb_da, wb_hg, wb_xa, wo)

        h, = _ffn(h, ffn2_norm[l], wg2, wu2, wd2_half, final_norm, final_norm=(l == depth - 1))
    return h.reshape(bsz, seq, d)
```
